```python
import math, functools
import jax, jax.numpy as jnp
from jax import lax
import numpy as np


D_MODEL = 4096
BATCH = 4
SEQ = 2048
DEPTH = 4
DEC_BATCH = 128
DEC_SEQ = 8
PAST_LEN = 8192
PAGE_SIZE = 128

N_MIXERS = 3
QBLOCK = 128
EPS = 1e-6
ADA_CHUNKS = 6
SB_HEAD_DIM = 128
SB_HEADS = D_MODEL // SB_HEAD_DIM
SB_KV_HEADS = 2
MLA_HEADS = D_MODEL // 128
MLA_NOPE = 128
MLA_ROPE = 64
MLA_V = 128
MLA_Q_RANK = D_MODEL // 4
MLA_KV_RANK = 512
MLA_SCALE = (MLA_NOPE + MLA_ROPE) ** -0.5
ROPE_THETA = 10000.0
DIFF_HEAD_DIM = 128
DIFF_HEADS = D_MODEL // (2 * DIFF_HEAD_DIM)
DIFF_KV_HEADS = 1
D_FF = ((8 * D_MODEL // 3 + 255) // 256) * 256

kernel_name = 'hybrid_sb_mla_diff_decoder_step'


def _rms(x, g):
    xf = x.astype(jnp.float32)
    y = xf * lax.rsqrt(jnp.mean(xf * xf, axis=-1, keepdims=True) + EPS)
    return (y * g.astype(jnp.float32)).astype(x.dtype)


def _rope(x, pos):
    half = x.shape[-1] // 2
    inv = ROPE_THETA ** (-jnp.arange(half, dtype=jnp.float32) / half)
    ang = pos.astype(jnp.float32)[:, None] * inv
    ang = ang.reshape((pos.shape[0],) + (1,) * (x.ndim - 3) + (half,))
    cos, sin = jnp.cos(ang), jnp.sin(ang)
    xf = x.astype(jnp.float32)
    x1, x2 = xf[..., :half], xf[..., half:]
    return jnp.concatenate([x1 * cos - x2 * sin, x1 * sin + x2 * cos], axis=-1).astype(x.dtype)


def _sweep(block_fn, q_parts, q_pos):
    t = q_pos.shape[0]
    qb = min(QBLOCK, t)
    nb = t // qb
    blocks = tuple(a.reshape((nb, qb) + a.shape[1:]) for a in q_parts)
    out = lax.map(lambda a: block_fn(*a[0], a[1]), (blocks, q_pos.reshape(nb, qb)))
    return out.reshape((t,) + out.shape[2:])


def _sb_core(q, k, v, q_pos, k_pos):
    def block(qi, pi):
        z = jnp.einsum('tkgd,skd->kgts', qi, k).astype(jnp.float32) * (SB_HEAD_DIM ** -0.5)
        strict = k_pos[None, :] < pi[:, None]
        log_stay = jnp.where(strict, jax.nn.log_sigmoid(-z), 0.0)
        log_after = lax.cumsum(log_stay, axis=3, reverse=True) - log_stay
        a = jnp.where(strict, jnp.exp(jax.nn.log_sigmoid(z) + log_after), 0.0)
        return jnp.einsum('kgts,skd->tkgd', a.astype(v.dtype), v)
    return _sweep(block, (q,), q_pos)


def _mla_core(q_lat, q_pe, ckv, kpe, q_pos, k_pos):
    def block(ql, qp, pi):
        s = (jnp.einsum('thr,sr->hts', ql, ckv) + jnp.einsum('thp,sp->hts', qp, kpe)).astype(jnp.float32) * MLA_SCALE
        s = jnp.where(k_pos[None, :] <= pi[:, None], s, -jnp.inf)
        p = jax.nn.softmax(s, axis=-1)
        return jnp.einsum('hts,sr->thr', p.astype(ckv.dtype), ckv)
    return _sweep(block, (q_lat, q_pe), q_pos)


def _diff_core(q, k, v, q_pos, k_pos, lam, slopes):
    k = k.reshape(k.shape[0], DIFF_KV_HEADS, 2, DIFF_HEAD_DIM)
    def block(qi, pi):
        s = jnp.einsum('tkgid,skid->ikgts', qi, k).astype(jnp.float32) * (DIFF_HEAD_DIM ** -0.5)
        dist = (pi[:, None] - k_pos[None, :]).astype(jnp.float32)
        s = s - slopes[None, :, :, None, None] * dist
        s = jnp.where(dist >= 0, s, -jnp.inf)
        p = jax.nn.softmax(s, axis=-1)
        a = p[0] - lam * p[1]
        return jnp.einsum('kgts,skv->tkgv', a.astype(v.dtype), v)
    return _sweep(block, (q,), q_pos)


def _gather_pages(pool, row):
    return pool[row].reshape((-1,) + pool.shape[2:])


def _positions(t, page_table):
    past = 0 if page_table is None else page_table.shape[1] * PAGE_SIZE
    q_pos = past + jnp.arange(t, dtype=jnp.int32)
    k_pos = jnp.arange(past + t, dtype=jnp.int32)
    return q_pos, k_pos


def _attend(core, qs, kvs, pools, page_table, q_pos, k_pos):
    if pools is None:
        return lax.map(lambda a: core(*a[0], *a[1], q_pos, k_pos), (qs, kvs))

    def one_seq(a):
        qs_i, kvs_i, row = a
        full = tuple(jnp.concatenate([_gather_pages(p, row), n.astype(p.dtype)], axis=0)
                     for p, n in zip(pools, kvs_i))
        return core(*qs_i, *full, q_pos, k_pos)
    return lax.map(one_seq, (qs, kvs, page_table))


def _sb_mixer(h, j, pools, page_table, W):
    b, t, _ = h.shape
    g = SB_HEADS // SB_KV_HEADS
    nq, nk = SB_HEADS * SB_HEAD_DIM, SB_KV_HEADS * SB_HEAD_DIM
    proj = h @ W['w_in_sb'][j]
    q = proj[..., :nq].reshape(b, t, SB_KV_HEADS, g, SB_HEAD_DIM)
    k = proj[..., nq:nq + nk].reshape(b, t, SB_KV_HEADS, SB_HEAD_DIM)
    v = proj[..., nq + nk:].reshape(b, t, SB_KV_HEADS, SB_HEAD_DIM)
    q_pos, k_pos = _positions(t, page_table)
    o = _attend(_sb_core, (q,), (k, v), pools, page_table, q_pos, k_pos)
    return o.reshape(b, t, nq) @ W['w_out_sb'][j], k, v


def _mla_mixer(h, j, pools, page_table, W):
    b, t, _ = h.shape
    dq = MLA_NOPE + MLA_ROPE
    proj = h @ W['w_in_mla'][j]
    q_a = proj[..., :MLA_Q_RANK]
    kv_a = proj[..., MLA_Q_RANK:MLA_Q_RANK + MLA_KV_RANK]
    k_pe = proj[..., MLA_Q_RANK + MLA_KV_RANK:]
    q_pos, k_pos = _positions(t, page_table)
    q = (_rms(q_a, W['g_q_a'][j]) @ W['w_q_b'][j]).reshape(b, t, MLA_HEADS, dq)
    q = _rms(q, W['g_q_head'][j])
    q_pe = _rope(q[..., MLA_NOPE:], q_pos)
    ckv = _rms(kv_a, W['g_kv_a'][j])
    kpe = _rope(_rms(k_pe, W['g_k_pe'][j]), q_pos)
    w_kv_b = W['w_kv_b'][j]
    q_lat = jnp.einsum('bthn,rhn->bthr', q[..., :MLA_NOPE], w_kv_b[..., :MLA_NOPE])
    ctx = _attend(_mla_core, (q_lat, q_pe), (ckv, kpe), pools, page_table, q_pos, k_pos)
    o = jnp.einsum('bthr,rhv->bthv', ctx, w_kv_b[..., MLA_NOPE:])
    return o.reshape(b, t, MLA_HEADS * MLA_V) @ W['w_out_mla'][j], ckv, kpe


def _diff_mixer(h, j, l, pools, page_table, W):
    b, t, _ = h.shape
    hd = DIFF_HEAD_DIM
    g = DIFF_HEADS // DIFF_KV_HEADS
    nq, nk = DIFF_HEADS * 2 * hd, DIFF_KV_HEADS * 2 * hd
    proj = h @ W['w_in_diff'][j]
    q = _rms(proj[..., :nq].reshape(b, t, DIFF_KV_HEADS, g, 2, hd), W['g_q_diff'][j])
    k = _rms(proj[..., nq:nq + nk].reshape(b, t, DIFF_KV_HEADS, 2, hd), W['g_k_diff'][j])
    k = k.reshape(b, t, DIFF_KV_HEADS, 2 * hd)
    v = proj[..., nq + nk:].reshape(b, t, DIFF_KV_HEADS, 2 * hd)
    lam_init = 0.8 - 0.6 * math.exp(-0.3 * l)
    ex = lambda a, c: jnp.exp(jnp.sum(a.astype(jnp.float32) * c.astype(jnp.float32)))
    lam = ex(W['lam_q1'][j], W['lam_k1'][j]) - ex(W['lam_q2'][j], W['lam_k2'][j]) + lam_init
    slopes = (2.0 ** (-8.0 * jnp.arange(1, DIFF_HEADS + 1, dtype=jnp.float32) / DIFF_HEADS)).reshape(DIFF_KV_HEADS, g)
    q_pos, k_pos = _positions(t, page_table)
    core = functools.partial(_diff_core, lam=lam, slopes=slopes)
    o = _attend(core, (q,), (k, v), pools, page_table, q_pos, k_pos)
    o = _rms(o, W['g_subln'][j]) * (1.0 - lam_init)
    return o.reshape(b, t, nq) @ W['w_out_diff'][j], k, v


def _mixer(l, h, pools, page_table, W):
    kind, j = l % N_MIXERS, l // N_MIXERS
    if kind == 0:
        return _sb_mixer(h, j, pools, page_table, W)
    if kind == 1:
        return _mla_mixer(h, j, pools, page_table, W)
    return _diff_mixer(h, j, l, pools, page_table, W)


def _ada_base(c, w_ada, b_ada):
    return (jax.nn.silu(c) @ w_ada + b_ada).reshape(c.shape[0], ADA_CHUNKS, D_MODEL)


def _layer(l, x, mod, ffn, pools, page_table, W):
    g_mix, g_ffn, w_gu, w_dn = ffn
    sh1, sc1, gt1, sh2, sc2, gt2 = (mod[:, i, None, :] for i in range(ADA_CHUNKS))
    h = _rms(x, g_mix) * (1.0 + sc1) + sh1
    o, st_a, st_b = _mixer(l, h, pools, page_table, W)
    x = x + gt1 * o
    h = _rms(x, g_ffn) * (1.0 + sc2) + sh2
    gu = h @ w_gu
    x = x + gt2 * ((jax.nn.silu(gu[..., :D_FF]) * gu[..., D_FF:]) @ w_dn)
    return x, (st_a, st_b)


def setup_inputs(seed: int = 0) -> dict:
    key = jax.random.key(seed)
    ks = iter(jax.random.split(key, 64))
    f32 = jnp.float32

    def nrm(shape, scale=1.0):
        return jax.random.normal(next(ks), shape, f32) * scale

    def gain(shape):
        return 1.0 + nrm(shape, 0.01)

    n_pages = PAST_LEN // PAGE_SIZE
    in_use = DEC_BATCH * n_pages
    n_pool = in_use + max(1, in_use // 4)
    n_sb, n_mla, n_diff = [(DEPTH - kind + N_MIXERS - 1) // N_MIXERS for kind in range(N_MIXERS)]
    sb_nq, sb_nk = SB_HEADS * SB_HEAD_DIM, SB_KV_HEADS * SB_HEAD_DIM
    d_nq, d_nk = DIFF_HEADS * 2 * DIFF_HEAD_DIM, DIFF_KV_HEADS * 2 * DIFF_HEAD_DIM
    sb_row = (n_pool, PAGE_SIZE, SB_KV_HEADS, SB_HEAD_DIM)
    diff_row = (n_pool, PAGE_SIZE, DIFF_KV_HEADS, 2 * DIFF_HEAD_DIM)
    return {
        'x_prompt': nrm((BATCH, SEQ, D_MODEL)),
        'x_sample': nrm((DEC_BATCH, DEC_SEQ, D_MODEL)),
        'c_prompt': nrm((BATCH, D_MODEL)),
        'c_sample': nrm((DEC_BATCH, D_MODEL)),
        'cache_l0_k': nrm(sb_row),
        'cache_l0_v': nrm(sb_row),
        'cache_l1_ckv': nrm((n_pool, PAGE_SIZE, MLA_KV_RANK)),
        'cache_l1_kpe': nrm((n_pool, PAGE_SIZE, MLA_ROPE)),
        'cache_l2_k': nrm(diff_row),
        'cache_l2_v': nrm(diff_row),
        'cache_l3_k': nrm(sb_row),
        'cache_l3_v': nrm(sb_row),
        'page_table': jax.random.permutation(next(ks), n_pool)[:in_use].reshape(DEC_BATCH, n_pages).astype(jnp.int32),
        'w_ada': nrm((D_MODEL, ADA_CHUNKS * D_MODEL), D_MODEL ** -0.5),
        'b_ada': nrm((ADA_CHUNKS * D_MODEL,), 0.01),
        'ada_table': nrm((DEPTH, ADA_CHUNKS, D_MODEL), 0.02),
        'g_norm_mix': gain((DEPTH, D_MODEL)),
        'g_norm_ffn': gain((DEPTH, D_MODEL)),
        'w_in_sb': nrm((n_sb, D_MODEL, sb_nq + 2 * sb_nk), D_MODEL ** -0.5),
        'w_out_sb': nrm((n_sb, sb_nq, D_MODEL), sb_nq ** -0.5),
        'w_in_mla': nrm((n_mla, D_MODEL, MLA_Q_RANK + MLA_KV_RANK + MLA_ROPE), D_MODEL ** -0.5),
        'g_q_a': gain((n_mla, MLA_Q_RANK)),
        'g_kv_a': gain((n_mla, MLA_KV_RANK)),
        'g_k_pe': gain((n_mla, MLA_ROPE)),
        'w_q_b': nrm((n_mla, MLA_Q_RANK, MLA_HEADS * (MLA_NOPE + MLA_ROPE)), MLA_Q_RANK ** -0.5),
        'g_q_head': gain((n_mla, MLA_NOPE + MLA_ROPE)),
        'w_kv_b': nrm((n_mla, MLA_KV_RANK, MLA_HEADS, MLA_NOPE + MLA_V), MLA_KV_RANK ** -0.5),
        'w_out_mla': nrm((n_mla, MLA_HEADS * MLA_V, D_MODEL), (MLA_HEADS * MLA_V) ** -0.5),
        'w_in_diff': nrm((n_diff, D_MODEL, d_nq + 2 * d_nk), D_MODEL ** -0.5),
        'g_q_diff': gain((n_diff, DIFF_HEAD_DIM)),
        'g_k_diff': gain((n_diff, DIFF_HEAD_DIM)),
        'lam_q1': nrm((n_diff, DIFF_HEAD_DIM), 0.1),
        'lam_k1': nrm((n_diff, DIFF_HEAD_DIM), 0.1),
        'lam_q2': nrm((n_diff, DIFF_HEAD_DIM), 0.1),
        'lam_k2': nrm((n_diff, DIFF_HEAD_DIM), 0.1),
        'g_subln': gain((n_diff, 2 * DIFF_HEAD_DIM)),
        'w_out_diff': nrm((n_diff, d_nq, D_MODEL), d_nq ** -0.5),
        'w_gate_up': nrm((DEPTH, D_MODEL, 2 * D_FF), D_MODEL ** -0.5),
        'w_down': nrm((DEPTH, D_FF, D_MODEL), D_FF ** -0.5),
    }


def reference(x_prompt, x_sample, c_prompt, c_sample, cache_l0_k, cache_l0_v, cache_l1_ckv, cache_l1_kpe,
              cache_l2_k, cache_l2_v, cache_l3_k, cache_l3_v, page_table, w_ada, b_ada, ada_table,
              g_norm_mix, g_norm_ffn, w_in_sb, w_out_sb, w_in_mla, g_q_a, g_kv_a, g_k_pe, w_q_b, g_q_head,
              w_kv_b, w_out_mla, w_in_diff, g_q_diff, g_k_diff, lam_q1, lam_k1, lam_q2, lam_k2, g_subln,
              w_out_diff, w_gate_up, w_down):
    W = {'w_in_sb': w_in_sb, 'w_out_sb': w_out_sb,
         'w_in_mla': w_in_mla, 'g_q_a': g_q_a, 'g_kv_a': g_kv_a, 'g_k_pe': g_k_pe, 'w_q_b': w_q_b,
         'g_q_head': g_q_head, 'w_kv_b': w_kv_b, 'w_out_mla': w_out_mla,
         'w_in_diff': w_in_diff, 'g_q_diff': g_q_diff, 'g_k_diff': g_k_diff, 'lam_q1': lam_q1,
         'lam_k1': lam_k1, 'lam_q2': lam_q2, 'lam_k2': lam_k2, 'g_subln': g_subln, 'w_out_diff': w_out_diff}
    pools = ((cache_l0_k, cache_l0_v), (cache_l1_ckv, cache_l1_kpe),
             (cache_l2_k, cache_l2_v), (cache_l3_k, cache_l3_v))
    mod_p = _ada_base(c_prompt, w_ada, b_ada)
    mod_s = _ada_base(c_sample, w_ada, b_ada)
    x_p, x_s = x_prompt, x_sample
    new_p, new_s = [], []
    for l in range(DEPTH):
        ffn = (g_norm_mix[l], g_norm_ffn[l], w_gate_up[l], w_down[l])
        x_p, st = _layer(l, x_p, mod_p + ada_table[l], ffn, None, None, W)
        new_p.append(st)
        x_s, st = _layer(l, x_s, mod_s + ada_table[l], ffn, pools[l], page_table, W)
        new_s.append(st)
    return (x_p, x_s,
            new_p[0][0], new_p[0][1], new_p[1][0], new_p[1][1],
            new_p[2][0], new_p[2][1], new_p[3][0], new_p[3][1],
            new_s[0][0], new_s[0][1], new_s[1][0], new_s[1][1],
            new_s[2][0], new_s[2][1], new_s[3][0], new_s[3][1])
```

```python
import functools
import math

import jax
import jax.numpy as jnp
from jax import lax
from jax.experimental import pallas as pl
from jax.experimental.pallas import tpu as pltpu

F32 = jnp.float32
BF16 = jnp.bfloat16

HEAD_DIM = 128
EPS = 1e-6
ROPE_THETA = 10000.0
NEG_BIG = -1e30
VMEM_LIMIT_BYTES = 56 * 1024 * 1024
Q_BLOCK = 256
KEY_CHUNK = 256
PAGES_PER_STEP = 8


def _params(*sem):
    return pltpu.CompilerParams(dimension_semantics=sem, vmem_limit_bytes=VMEM_LIMIT_BYTES)


def _tile(n, want, align):
    if n <= want:
        return n
    t = (want // align) * align
    while t > 0 and n % t:
        t -= align
    assert t > 0, (n, want, align)
    return t


def _dot(a, b):
    return jnp.dot(a, b, preferred_element_type=F32)


def _dot_t(a, b):
    return lax.dot_general(a, b, (((1,), (1,)), ((), ())), preferred_element_type=F32)


def _mm_kernel(a_ref, w_ref, o_ref, *, a_slices, o_slices):
    if a_slices:
        a = jnp.concatenate([a_ref[h].astype(BF16) for h in range(a_slices)], axis=-1)
    else:
        a = a_ref[...].astype(BF16)
    acc = _dot(a, w_ref[...])
    if o_slices:
        for h in range(o_slices):
            o_ref[h] = acc[:, h * HEAD_DIM:(h + 1) * HEAD_DIM].astype(o_ref.dtype)
    else:
        o_ref[...] = acc.astype(o_ref.dtype)


def _mm(a, w, *, out_dtype, tm, tn, a_head_major=False, out_head_major=False, name="mm"):
    if a_head_major:
        s_in, m, _ = a.shape
        k = s_in * HEAD_DIM
    else:
        m, k = a.shape
        s_in = 0
    n = w.shape[1]
    assert w.shape[0] == k
    tm, tn = _tile(m, tm, 8), _tile(n, tn, HEAD_DIM)
    if a_head_major:
        a_spec = pl.BlockSpec((s_in, tm, HEAD_DIM), lambda i, j: (0, i, 0))
    else:
        a_spec = pl.BlockSpec((tm, k), lambda i, j: (i, 0))
    w_spec = pl.BlockSpec((k, tn), lambda i, j: (0, j))
    if out_head_major:
        assert tn % HEAD_DIM == 0
        s_out = tn // HEAD_DIM
        out_shape = jax.ShapeDtypeStruct((n // HEAD_DIM, m, HEAD_DIM), out_dtype)
        o_spec = pl.BlockSpec((s_out, tm, HEAD_DIM), lambda i, j: (j, i, 0))
    else:
        s_out = 0
        out_shape = jax.ShapeDtypeStruct((m, n), out_dtype)
        o_spec = pl.BlockSpec((tm, tn), lambda i, j: (i, j))
    return pl.pallas_call(
        functools.partial(_mm_kernel, a_slices=s_in, o_slices=s_out),
        grid=(m // tm, n // tn),
        in_specs=[a_spec, w_spec],
        out_specs=o_spec,
        out_shape=out_shape,
        compiler_params=_params("parallel", "arbitrary"),
        name=name,
    )(a, w)


def _bmm_kernel(a_ref, w_ref, o_ref):
    o_ref[...] = _dot(a_ref[...].astype(BF16), w_ref[...]).astype(o_ref.dtype)


def _bmm(a, w, *, out_dtype, tm, name="bmm"):
    h, m, k = a.shape
    n = w.shape[2]
    tm = _tile(m, tm, 8)
    return pl.pallas_call(
        _bmm_kernel,
        grid=(h, m // tm),
        in_specs=[pl.BlockSpec((None, tm, k), lambda g, i: (g, i, 0)),
                  pl.BlockSpec((None, k, n), lambda g, i: (g, 0, 0))],
        out_specs=pl.BlockSpec((None, tm, n), lambda g, i: (g, i, 0)),
        out_shape=jax.ShapeDtypeStruct((h, m, n), out_dtype),
        compiler_params=_params("parallel", "arbitrary"),
        name=name,
    )(a, w)


def _ffn_kernel(h_ref, wg_ref, wu_ref, wd_ref, o_ref):
    j = pl.program_id(1)
    h = h_ref[...]
    g = _dot(h, wg_ref[...])
    u = _dot(h, wu_ref[...])
    act = (g * (1.0 / (1.0 + jnp.exp(-g)))) * u
    part = _dot(act.astype(BF16), wd_ref[...])

    @pl.when(j == 0)
    def _():
        o_ref[...] = part

    @pl.when(j > 0)
    def _():
        o_ref[...] += part


def _ffn(h, w_gu, w_dn, *, tm, tf):
    m, d = h.shape
    d_ff = w_dn.shape[0]
    tm = _tile(m, tm, 8)
    assert d_ff % tf == 0
    nf = d_ff // tf
    return pl.pallas_call(
        _ffn_kernel,
        grid=(m // tm, nf),
        in_specs=[pl.BlockSpec((tm, d), lambda i, j: (i, 0)),
                  pl.BlockSpec((d, tf), lambda i, j: (0, j)),
                  pl.BlockSpec((d, tf), lambda i, j: (0, nf + j)),
                  pl.BlockSpec((tf, d), lambda i, j: (j, 0))],
        out_specs=pl.BlockSpec((tm, d), lambda i, j: (i, 0)),
        out_shape=jax.ShapeDtypeStruct((m, d), F32),
        compiler_params=_params("parallel", "arbitrary"),
        name="ffn",
    )(h, w_gu, w_gu, w_dn)


def _norm_kernel(*refs, has_resid, do_norm):
    refs = list(refs)
    x_ref = refs.pop(0)
    x = x_ref[...]
    if has_resid:
        o_ref = refs.pop(0)
        gt_ref = refs.pop(0)
        x = x + gt_ref[...] * o_ref[...]
    if do_norm:
        g_ref, sc_ref, sh_ref, xo_ref, h_ref = refs
    else:
        (xo_ref,) = refs
    xo_ref[...] = x
    if do_norm:
        ms = jnp.mean(x * x, axis=-1, keepdims=True)
        y = (x * lax.rsqrt(ms + EPS)) * g_ref[...]
        hm = y * (1.0 + sc_ref[...]) + sh_ref[...]
        bb, tt, d = hm.shape
        h_ref[...] = hm.reshape(bb * tt, d).astype(BF16)


def _norm_stage(x, o, gt, g, sc, sh, *, bb, tt):
    b, t, d = x.shape
    bb, tt = min(bb, b), min(tt, t)
    assert b % bb == 0 and t % tt == 0
    nt = t // tt
    has_resid = o is not None
    do_norm = g is not None
    big = pl.BlockSpec((bb, tt, d), lambda i, j: (i, j, 0))
    per_seq = pl.BlockSpec((bb, 1, d), lambda i, j: (i, 0, 0))
    args, specs = [x], [big]
    if has_resid:
        args += [o, gt]
        specs += [big, per_seq]
    out_shape = [jax.ShapeDtypeStruct((b, t, d), F32)]
    out_specs = [big]
    if do_norm:
        args += [g, sc, sh]
        specs += [pl.BlockSpec((1, 1, d), lambda i, j: (0, 0, 0)), per_seq, per_seq]
        out_shape.append(jax.ShapeDtypeStruct((b * t, d), BF16))
        out_specs.append(pl.BlockSpec((bb * tt, d), lambda i, j: (i * nt + j, 0)))
    res = pl.pallas_call(
        functools.partial(_norm_kernel, has_resid=has_resid, do_norm=do_norm),
        grid=(b // bb, nt),
        in_specs=specs,
        out_specs=out_specs,
        out_shape=out_shape,
        compiler_params=_params("parallel", "parallel"),
        name="norm_stage",
    )(*args)
    return (res[0], res[1]) if do_norm else (res[0], None)


def _sb_update(q, k, v, tri, strict, acc, carry):
    z = _dot_t(q, k) * (HEAD_DIM ** -0.5)
    lsig = jnp.minimum(z, 0.0) - jnp.log(1.0 + jnp.exp(-jnp.abs(z)))
    ls = lsig - z
    if strict is not None:
        ls = jnp.where(strict, ls, 0.0)
    hi = ls.astype(BF16)
    lo = (ls - hi.astype(F32)).astype(BF16)
    la = _dot(hi, tri) + _dot(lo, tri)
    a = jnp.exp(lsig + la + carry)
    if strict is not None:
        a = jnp.where(strict, a, 0.0)
    acc = acc + _dot(a.astype(BF16), v)
    carry = carry + jnp.sum(ls, axis=-1, keepdims=True)
    return acc, carry


def _softmax_update(s, v, m, l, acc):
    m_new = jnp.maximum(m, jnp.max(s, axis=-1, keepdims=True))
    alpha = jnp.exp(m - m_new)
    p = jnp.exp(s - m_new)
    l = alpha * l + jnp.sum(p, axis=-1, keepdims=True)
    acc = alpha * acc + _dot(p.astype(BF16), v)
    return m_new, l, acc


def _rms_rows(x, g):
    return (x * lax.rsqrt(jnp.mean(x * x, axis=-1, keepdims=True) + EPS)) * g


def _causal_pairs(n_blocks):
    qi, ci = [], []
    for q in range(n_blocks):
        for c in range(q, -1, -1):
            qi.append(q)
            ci.append(c)
    return jnp.asarray(qi, jnp.int32), jnp.asarray(ci, jnp.int32)


def _block_positions(qi, c, tq, ck):
    qpos = qi * tq + lax.broadcasted_iota(jnp.int32, (tq, 1), 0)
    kpos = c * ck + lax.broadcasted_iota(jnp.int32, (1, ck), 1)
    return qpos, kpos


def _sb_prompt_kernel(qi_ref, ci_ref, q_ref, k_ref, v_ref, tri_ref, o_ref, acc_ref, carry_ref,
                      *, tq, ck, group):
    p = pl.program_id(2)
    qi, c = qi_ref[p], ci_ref[p]

    @pl.when(c == qi)
    def _():
        acc_ref[...] = jnp.zeros_like(acc_ref)
        carry_ref[...] = jnp.zeros_like(carry_ref)

    qpos, kpos = _block_positions(qi, c, tq, ck)
    strict = kpos < qpos
    k = k_ref[...].astype(BF16)
    v = v_ref[...].astype(BF16)
    tri = tri_ref[...]

    def head(g, _):
        acc, carry = _sb_update(q_ref[g].astype(BF16), k, v, tri, strict, acc_ref[g], carry_ref[g])
        acc_ref[g] = acc
        carry_ref[g] = carry
        return 0

    lax.fori_loop(0, group, head, 0)

    @pl.when(c == 0)
    def _():
        o_ref[...] = acc_ref[...].astype(o_ref.dtype)


def _sb_prompt_attn(proj_hm, tri, *, batch, seq, n_q, n_kv):
    tq = ck = min(Q_BLOCK, seq)
    assert seq % tq == 0
    nb = seq // tq
    group = n_q // n_kv
    qi_tab, ci_tab = _causal_pairs(nb)
    grid_spec = pltpu.PrefetchScalarGridSpec(
        num_scalar_prefetch=2,
        grid=(batch, n_kv, int(qi_tab.shape[0])),
        in_specs=[
            pl.BlockSpec((group, tq, HEAD_DIM), lambda b, h, p, qi, ci: (h, b * nb + qi[p], 0)),
            pl.BlockSpec((None, ck, HEAD_DIM), lambda b, h, p, qi, ci: (n_q + h, b * nb + ci[p], 0)),
            pl.BlockSpec((None, ck, HEAD_DIM), lambda b, h, p, qi, ci: (n_q + n_kv + h, b * nb + ci[p], 0)),
            pl.BlockSpec((ck, ck), lambda b, h, p, qi, ci: (0, 0)),
        ],
        out_specs=pl.BlockSpec((group, tq, HEAD_DIM), lambda b, h, p, qi, ci: (h, b * nb + qi[p], 0)),
        scratch_shapes=[pltpu.VMEM((group, tq, HEAD_DIM), F32), pltpu.VMEM((group, tq, 1), F32)],
    )
    return pl.pallas_call(
        functools.partial(_sb_prompt_kernel, tq=tq, ck=ck, group=group),
        grid_spec=grid_spec,
        out_shape=jax.ShapeDtypeStruct((n_q, batch * seq, HEAD_DIM), BF16),
        compiler_params=_params("parallel", "parallel", "arbitrary"),
        name="sb_prompt_attn",
    )(qi_tab, ci_tab, proj_hm, proj_hm, proj_hm, tri)


def _diff_prompt_kernel(qi_ref, ci_ref, par_ref, q_ref, k_ref, v_ref, gq_ref, gs_ref, o_ref,
                        m_ref, l_ref, acc_ref, *, tq, ck, group, head_base, out_scale):
    p = pl.program_id(2)
    hg = pl.program_id(1)
    qi, c = qi_ref[p], ci_ref[p]

    @pl.when(c == qi)
    def _():
        m_ref[...] = jnp.full_like(m_ref, NEG_BIG)
        l_ref[...] = jnp.zeros_like(l_ref)
        acc_ref[...] = jnp.zeros_like(acc_ref)

    qpos, kpos = _block_positions(qi, c, tq, ck)
    dist = (qpos - kpos).astype(F32)
    valid = dist >= 0.0
    v = jnp.concatenate([v_ref[0], v_ref[1]], axis=-1).astype(BF16)
    ks = (k_ref[0].astype(BF16), k_ref[1].astype(BF16))
    gq = gq_ref[...]

    def head(g, _):
        slope = par_ref[head_base + hg * group + g]
        bias = slope * dist
        for i in range(2):
            qn = _rms_rows(q_ref[2 * g + i], gq).astype(BF16)
            s = _dot_t(qn, ks[i]) * (HEAD_DIM ** -0.5) - bias
            s = jnp.where(valid, s, -jnp.inf)
            m, l, acc = _softmax_update(s, v, m_ref[i, g], l_ref[i, g], acc_ref[i, g])
            m_ref[i, g] = m
            l_ref[i, g] = l
            acc_ref[i, g] = acc
        return 0

    lax.fori_loop(0, group, head, 0)

    @pl.when(c == 0)
    def _():
        lam = par_ref[0]
        gs = gs_ref[...]

        def fin(g, _):
            o = acc_ref[0, g] / l_ref[0, g] - lam * (acc_ref[1, g] / l_ref[1, g])
            o = _rms_rows(o, gs) * out_scale
            o_ref[2 * g] = o[:, :HEAD_DIM].astype(o_ref.dtype)
            o_ref[2 * g + 1] = o[:, HEAD_DIM:].astype(o_ref.dtype)
            return 0

        lax.fori_loop(0, group, fin, 0)


def _diff_prompt_attn(proj_hm, k_hm, par, gq, gs, *, batch, seq, n_heads, out_scale):
    tq = ck = min(Q_BLOCK, seq)
    assert seq % tq == 0
    nb = seq // tq
    n_groups = 2 if n_heads % 2 == 0 else 1
    group = n_heads // n_groups
    qi_tab, ci_tab = _causal_pairs(nb)
    v_blk = (2 * n_heads + 2) // 2
    grid_spec = pltpu.PrefetchScalarGridSpec(
        num_scalar_prefetch=2,
        grid=(batch, n_groups, int(qi_tab.shape[0])),
        in_specs=[
            pl.BlockSpec(memory_space=pltpu.SMEM),
            pl.BlockSpec((2 * group, tq, HEAD_DIM), lambda b, h, p, qi, ci: (h, b * nb + qi[p], 0)),
            pl.BlockSpec((2, ck, HEAD_DIM), lambda b, h, p, qi, ci: (0, b * nb + ci[p], 0)),
            pl.BlockSpec((2, ck, HEAD_DIM), lambda b, h, p, qi, ci: (v_blk, b * nb + ci[p], 0)),
            pl.BlockSpec((1, HEAD_DIM), lambda b, h, p, qi, ci: (0, 0)),
            pl.BlockSpec((1, 2 * HEAD_DIM), lambda b, h, p, qi, ci: (0, 0)),
        ],
        out_specs=pl.BlockSpec((2 * group, tq, HEAD_DIM), lambda b, h, p, qi, ci: (h, b * nb + qi[p], 0)),
        scratch_shapes=[pltpu.VMEM((2, group, tq, 1), F32), pltpu.VMEM((2, group, tq, 1), F32),
                        pltpu.VMEM((2, group, tq, 2 * HEAD_DIM), F32)],
    )
    return pl.pallas_call(
        functools.partial(_diff_prompt_kernel, tq=tq, ck=ck, group=group, head_base=1,
                          out_scale=out_scale),
        grid_spec=grid_spec,
        out_shape=jax.ShapeDtypeStruct((2 * n_heads, batch * seq, HEAD_DIM), BF16),
        compiler_params=_params("parallel", "parallel", "arbitrary"),
        name="diff_prompt_attn",
    )(qi_tab, ci_tab, par, proj_hm, k_hm, proj_hm, gq, gs)


def _mla_prompt_kernel(qi_ref, ci_ref, ql_ref, qp_ref, ckv_ref, kpe_ref, o_ref, m_ref, l_ref, acc_ref,
                       *, tq, ck, group, scale):
    p = pl.program_id(2)
    qi, c = qi_ref[p], ci_ref[p]

    @pl.when(c == qi)
    def _():
        m_ref[...] = jnp.full_like(m_ref, NEG_BIG)
        l_ref[...] = jnp.zeros_like(l_ref)
        acc_ref[...] = jnp.zeros_like(acc_ref)

    qpos, kpos = _block_positions(qi, c, tq, ck)
    valid = kpos <= qpos
    ckv = ckv_ref[...].astype(BF16)
    kpe = kpe_ref[...].astype(BF16)

    def head(g, _):
        s = (_dot_t(ql_ref[g], ckv) + _dot_t(qp_ref[g].astype(BF16), kpe)) * scale
        s = jnp.where(valid, s, -jnp.inf)
        m, l, acc = _softmax_update(s, ckv, m_ref[g], l_ref[g], acc_ref[g])
        m_ref[g] = m
        l_ref[g] = l
        acc_ref[g] = acc
        return 0

    lax.fori_loop(0, group, head, 0)

    @pl.when(c == 0)
    def _():
        o_ref[...] = (acc_ref[...] / l_ref[...]).astype(o_ref.dtype)


def _mla_prompt_attn(q_lat, q_pe, ckv, kpe, *, batch, seq, scale):
    n_heads, _, rank = q_lat.shape
    rope = q_pe.shape[2]
    tq = ck = min(Q_BLOCK, seq)
    assert seq % tq == 0
    nb = seq // tq
    n_groups = 2 if n_heads % 2 == 0 else 1
    group = n_heads // n_groups
    qi_tab, ci_tab = _causal_pairs(nb)
    grid_spec = pltpu.PrefetchScalarGridSpec(
        num_scalar_prefetch=2,
        grid=(batch, n_groups, int(qi_tab.shape[0])),
        in_specs=[
            pl.BlockSpec((group, tq, rank), lambda b, h, p, qi, ci: (h, b * nb + qi[p], 0)),
            pl.BlockSpec((group, tq, rope), lambda b, h, p, qi, ci: (h, b * nb + qi[p], 0)),
            pl.BlockSpec((ck, rank), lambda b, h, p, qi, ci: (b * nb + ci[p], 0)),
            pl.BlockSpec((ck, rope), lambda b, h, p, qi, ci: (b * nb + ci[p], 0)),
        ],
        out_specs=pl.BlockSpec((group, tq, rank), lambda b, h, p, qi, ci: (h, b * nb + qi[p], 0)),
        scratch_shapes=[pltpu.VMEM((group, tq, 1), F32), pltpu.VMEM((group, tq, 1), F32),
                        pltpu.VMEM((group, tq, rank), F32)],
    )
    return pl.pallas_call(
        functools.partial(_mla_prompt_kernel, tq=tq, ck=ck, group=group, scale=scale),
        grid_spec=grid_spec,
        out_shape=jax.ShapeDtypeStruct((n_heads, batch * seq, rank), BF16),
        compiler_params=_params("parallel", "parallel", "arbitrary"),
        name="mla_prompt_attn",
    )(qi_tab, ci_tab, q_lat, q_pe, ckv, kpe)


def _pages_per_step(n_pages):
    p = min(PAGES_PER_STEP, n_pages)
    while n_pages % p or p % 2:
        p -= 1
    assert p >= 2, "page count must be even"
    return p


def _page_specs(block, n, step_page0):
    trail = (0,) * (len(block) - 1)
    return [pl.BlockSpec(block, (lambda b, s, pt, r=r: (pt[b, step_page0(s) + r],) + trail))
            for r in range(n)]


def _sb_sample_kernel(pt_ref, q_ref, kn_ref, vn_ref, tri_ref, *refs, n_pg, n_kv, group, t_dec, n_steps):
    kp, vp = refs[:n_pg], refs[n_pg:2 * n_pg]
    o_ref, acc_ref, carry_ref, kbuf, vbuf = refs[2 * n_pg:]
    s = pl.program_id(1)
    rows = group * t_dec
    page = kp[0].shape[0] // n_kv
    tri = tri_ref[...]

    def q_rows(h):
        return q_ref[h * group:(h + 1) * group].reshape(rows, HEAD_DIM).astype(BF16)

    @pl.when(s == 0)
    def _():
        t_row = lax.rem(lax.broadcasted_iota(jnp.int32, (rows, 1), 0), t_dec)
        col = lax.broadcasted_iota(jnp.int32, (1, kbuf.shape[0]), 1)
        strict = col < t_row
        for h in range(n_kv):
            kbuf[...] = jnp.zeros_like(kbuf)
            vbuf[...] = jnp.zeros_like(vbuf)
            kbuf[0:t_dec] = kn_ref[h]
            vbuf[0:t_dec] = vn_ref[h]
            acc, carry = _sb_update(q_rows(h), kbuf[...].astype(BF16), vbuf[...].astype(BF16), tri, strict,
                                    jnp.zeros((rows, HEAD_DIM), F32), jnp.zeros((rows, 1), F32))
            acc_ref[h] = acc
            carry_ref[h] = carry

    for j in range(n_pg // 2 - 1, -1, -1):
        for h in range(n_kv):
            sel = pl.ds(h, page, stride=n_kv)
            k = jnp.concatenate([kp[2 * j][sel, :], kp[2 * j + 1][sel, :]], axis=0).astype(BF16)
            v = jnp.concatenate([vp[2 * j][sel, :], vp[2 * j + 1][sel, :]], axis=0).astype(BF16)
            acc, carry = _sb_update(q_rows(h), k, v, tri, None, acc_ref[h], carry_ref[h])
            acc_ref[h] = acc
            carry_ref[h] = carry

    @pl.when(s == n_steps - 1)
    def _():
        for h in range(n_kv):
            o_ref[h * group:(h + 1) * group] = acc_ref[h].reshape(group, t_dec, HEAD_DIM).astype(o_ref.dtype)


def _sb_sample_attn(proj_hm, pool_k, pool_v, page_table, tri, *, batch, t_dec, n_q, n_kv):
    n_pages = page_table.shape[1]
    n_pg = _pages_per_step(n_pages)
    n_steps = n_pages // n_pg
    group = n_q // n_kv
    rows = group * t_dec
    ck = tri.shape[0]
    pg_block = (None,) + pool_k.shape[1:]
    page0 = lambda s: (n_steps - 1 - s) * n_pg
    grid_spec = pltpu.PrefetchScalarGridSpec(
        num_scalar_prefetch=1,
        grid=(batch, n_steps),
        in_specs=[
            pl.BlockSpec((n_q, t_dec, HEAD_DIM), lambda b, s, pt: (0, b, 0)),
            pl.BlockSpec((n_kv, t_dec, HEAD_DIM), lambda b, s, pt: (n_q // n_kv, b, 0)),
            pl.BlockSpec((n_kv, t_dec, HEAD_DIM), lambda b, s, pt: (n_q // n_kv + 1, b, 0)),
            pl.BlockSpec((ck, ck), lambda b, s, pt: (0, 0)),
        ] + _page_specs(pg_block, n_pg, page0) + _page_specs(pg_block, n_pg, page0),
        out_specs=pl.BlockSpec((n_q, t_dec, HEAD_DIM), lambda b, s, pt: (0, b, 0)),
        scratch_shapes=[pltpu.VMEM((n_kv, rows, HEAD_DIM), F32), pltpu.VMEM((n_kv, rows, 1), F32),
                        pltpu.VMEM((ck, HEAD_DIM), F32), pltpu.VMEM((ck, HEAD_DIM), F32)],
    )
    return pl.pallas_call(
        functools.partial(_sb_sample_kernel, n_pg=n_pg, n_kv=n_kv, group=group, t_dec=t_dec, n_steps=n_steps),
        grid_spec=grid_spec,
        out_shape=jax.ShapeDtypeStruct((n_q, batch * t_dec, HEAD_DIM), F32),
        compiler_params=_params("parallel", "arbitrary"),
        name="sb_sample_attn",
    )(page_table, proj_hm, proj_hm, proj_hm, tri, *([pool_k] * n_pg), *([pool_v] * n_pg))


def _mla_sample_kernel(pt_ref, ql_ref, qp_ref, cn_ref, knt_ref, *refs, n_pg, n_heads, t_dec, n_steps, scale):
    cp, kp = refs[:n_pg], refs[n_pg:2 * n_pg]
    o_ref, m_ref, l_ref, acc_ref, cbuf = refs[2 * n_pg:]
    s = pl.program_id(1)
    rows = n_heads * t_dec
    ql = ql_ref[...].reshape(rows, ql_ref.shape[2]).astype(BF16)
    qp = qp_ref[...].reshape(rows, qp_ref.shape[2]).astype(BF16)

    @pl.when(s == 0)
    def _():
        m_ref[...] = jnp.full_like(m_ref, NEG_BIG)
        l_ref[...] = jnp.zeros_like(l_ref)
        acc_ref[...] = jnp.zeros_like(acc_ref)

    def update(ckv, kpe_t, valid):
        sc = (_dot_t(ql, ckv) + _dot(qp, kpe_t)) * scale
        if valid is not None:
            sc = jnp.where(valid, sc, -jnp.inf)
        m, l, acc = _softmax_update(sc, ckv, m_ref[...], l_ref[...], acc_ref[...])
        m_ref[...] = m
        l_ref[...] = l
        acc_ref[...] = acc

    for j in range(n_pg // 2):
        ckv = jnp.concatenate([cp[2 * j][...], cp[2 * j + 1][...]], axis=0).astype(BF16)
        kpe_t = jnp.concatenate([kp[2 * j][...], kp[2 * j + 1][...]], axis=1).astype(BF16)
        update(ckv, kpe_t, None)

    @pl.when(s == n_steps - 1)
    def _():
        t_row = lax.rem(lax.broadcasted_iota(jnp.int32, (rows, 1), 0), t_dec)
        col = lax.broadcasted_iota(jnp.int32, (1, cbuf.shape[0]), 1)
        valid = col <= t_row
        cbuf[...] = jnp.zeros_like(cbuf)
        cbuf[0:t_dec] = cn_ref[...]
        update(cbuf[...].astype(BF16), knt_ref[...].astype(BF16), valid)
        o = acc_ref[...] / l_ref[...]
        o_ref[...] = o.reshape(n_heads, t_dec, o.shape[1]).astype(o_ref.dtype)


def _mla_sample_attn(q_lat, q_pe, ckv_new, kpe_new_t, pool_ckv, pool_kpe_t, page_table, *, batch, t_dec, scale):
    n_heads, _, rank = q_lat.shape
    rope = q_pe.shape[2]
    n_pages = page_table.shape[1]
    page = pool_ckv.shape[1]
    n_pg = _pages_per_step(n_pages)
    n_steps = n_pages // n_pg
    rows = n_heads * t_dec
    page0 = lambda s: s * n_pg
    grid_spec = pltpu.PrefetchScalarGridSpec(
        num_scalar_prefetch=1,
        grid=(batch, n_steps),
        in_specs=[
            pl.BlockSpec((n_heads, t_dec, rank), lambda b, s, pt: (0, b, 0)),
            pl.BlockSpec((n_heads, t_dec, rope), lambda b, s, pt: (0, b, 0)),
            pl.BlockSpec((t_dec, rank), lambda b, s, pt: (b, 0)),
            pl.BlockSpec((None, rope, page), lambda b, s, pt: (b, 0, 0)),
        ] + _page_specs((None, page, rank), n_pg, page0) + _page_specs((None, rope, page), n_pg, page0),
        out_specs=pl.BlockSpec((n_heads, t_dec, rank), lambda b, s, pt: (0, b, 0)),
        scratch_shapes=[pltpu.VMEM((rows, 1), F32), pltpu.VMEM((rows, 1), F32), pltpu.VMEM((rows, rank), F32),
                        pltpu.VMEM((page, rank), F32)],
    )
    return pl.pallas_call(
        functools.partial(_mla_sample_kernel, n_pg=n_pg, n_heads=n_heads, t_dec=t_dec, n_steps=n_steps,
                          scale=scale),
        grid_spec=grid_spec,
        out_shape=jax.ShapeDtypeStruct((n_heads, batch * t_dec, rank), F32),
        compiler_params=_params("parallel", "arbitrary"),
        name="mla_sample_attn",
    )(page_table, q_lat, q_pe, ckv_new, kpe_new_t, *([pool_ckv] * n_pg), *([pool_kpe_t] * n_pg))


def _diff_sample_kernel(pt_ref, par_ref, q_ref, kn_ref, vn_ref, slope_ref, gq_ref, gs_ref, *refs,
                        n_pg, n_heads, t_dec, n_steps, past, out_scale):
    kp, vp = refs[:n_pg], refs[n_pg:2 * n_pg]
    o_ref, m_ref, l_ref, acc_ref, kbuf, vbuf = refs[2 * n_pg:]
    s = pl.program_id(1)
    rows = n_heads * t_dec
    page = kp[0].shape[0] // 2
    ck = 2 * page
    gq = gq_ref[...]
    q4 = q_ref[...].reshape(n_heads, 2, t_dec, HEAD_DIM)
    qs = [_rms_rows(q4[:, i].reshape(rows, HEAD_DIM), gq).astype(BF16) for i in range(2)]
    slope = slope_ref[...]
    t_row = lax.rem(lax.broadcasted_iota(jnp.int32, (rows, 1), 0), t_dec)
    col = lax.broadcasted_iota(jnp.int32, (1, ck), 1)

    @pl.when(s == 0)
    def _():
        m_ref[...] = jnp.full_like(m_ref, NEG_BIG)
        l_ref[...] = jnp.zeros_like(l_ref)
        acc_ref[...] = jnp.zeros_like(acc_ref)

    def update(k1, k2, v, dist, valid):
        bias = slope * dist
        for i, k in enumerate((k1, k2)):
            sc = _dot_t(qs[i], k) * (HEAD_DIM ** -0.5) - bias
            if valid is not None:
                sc = jnp.where(valid, sc, -jnp.inf)
            m, l, acc = _softmax_update(sc, v, m_ref[i], l_ref[i], acc_ref[i])
            m_ref[i] = m
            l_ref[i] = l
            acc_ref[i] = acc

    def halves(ref_a, ref_b, i):
        sel = pl.ds(i, page, stride=2)
        return jnp.concatenate([ref_a[sel, :], ref_b[sel, :]], axis=0)

    for j in range(n_pg // 2):
        a, b = 2 * j, 2 * j + 1
        k1 = halves(kp[a], kp[b], 0).astype(BF16)
        k2 = halves(kp[a], kp[b], 1).astype(BF16)
        v = jnp.concatenate([halves(vp[a], vp[b], 0), halves(vp[a], vp[b], 1)], axis=-1).astype(BF16)
        kpos = (s * n_pg + 2 * j) * page + col
        dist = (past + t_row - kpos).astype(F32)
        update(k1, k2, v, dist, None)

    @pl.when(s == n_steps - 1)
    def _():
        kbuf[...] = jnp.zeros_like(kbuf)
        vbuf[...] = jnp.zeros_like(vbuf)
        for i in range(2):
            kbuf[i, 0:t_dec] = kn_ref[i]
        vbuf[0:t_dec, 0:HEAD_DIM] = vn_ref[0]
        vbuf[0:t_dec, HEAD_DIM:] = vn_ref[1]
        dist = (t_row - col).astype(F32)
        update(kbuf[0].astype(BF16), kbuf[1].astype(BF16), vbuf[...].astype(BF16), dist, dist >= 0.0)
        lam = par_ref[0]
        o = acc_ref[0] / l_ref[0] - lam * (acc_ref[1] / l_ref[1])
        o = _rms_rows(o, gs_ref[...]) * out_scale
        o4 = jnp.stack([o[:, :HEAD_DIM].reshape(n_heads, t_dec, HEAD_DIM),
                        o[:, HEAD_DIM:].reshape(n_heads, t_dec, HEAD_DIM)], axis=1)
        o_ref[...] = o4.reshape(2 * n_heads, t_dec, HEAD_DIM).astype(o_ref.dtype)


def _diff_sample_attn(proj_hm, k_hm, pool_k, pool_v, page_table, par, slope_rows, gq, gs,
                      *, batch, t_dec, n_heads, out_scale):
    n_pages = page_table.shape[1]
    page = pool_k.shape[1] // 2
    n_pg = _pages_per_step(n_pages)
    n_steps = n_pages // n_pg
    rows = n_heads * t_dec
    ck = 2 * page
    page0 = lambda s: s * n_pg
    pg_block = (None, 2 * page, HEAD_DIM)
    grid_spec = pltpu.PrefetchScalarGridSpec(
        num_scalar_prefetch=1,
        grid=(batch, n_steps),
        in_specs=[
            pl.BlockSpec(memory_space=pltpu.SMEM),
            pl.BlockSpec((2 * n_heads, t_dec, HEAD_DIM), lambda b, s, pt: (0, b, 0)),
            pl.BlockSpec((2, t_dec, HEAD_DIM), lambda b, s, pt: (0, b, 0)),
            pl.BlockSpec((2, t_dec, HEAD_DIM), lambda b, s, pt: (n_heads + 1, b, 0)),
            pl.BlockSpec((rows, 1), lambda b, s, pt: (0, 0)),
            pl.BlockSpec((1, HEAD_DIM), lambda b, s, pt: (0, 0)),
            pl.BlockSpec((1, 2 * HEAD_DIM), lambda b, s, pt: (0, 0)),
        ] + _page_specs(pg_block, n_pg, page0) + _page_specs(pg_block, n_pg, page0),
        out_specs=pl.BlockSpec((2 * n_heads, t_dec, HEAD_DIM), lambda b, s, pt: (0, b, 0)),
        scratch_shapes=[pltpu.VMEM((2, rows, 1), F32), pltpu.VMEM((2, rows, 1), F32),
                        pltpu.VMEM((2, rows, 2 * HEAD_DIM), F32),
                        pltpu.VMEM((2, ck, HEAD_DIM), F32), pltpu.VMEM((ck, 2 * HEAD_DIM), F32)],
    )
    return pl.pallas_call(
        functools.partial(_diff_sample_kernel, n_pg=n_pg, n_heads=n_heads, t_dec=t_dec, n_steps=n_steps,
                          past=n_pages * page, out_scale=out_scale),
        grid_spec=grid_spec,
        out_shape=jax.ShapeDtypeStruct((2 * n_heads, batch * t_dec, HEAD_DIM), F32),
        compiler_params=_params("parallel", "arbitrary"),
        name="diff_sample_attn",
    )(page_table, par, proj_hm, k_hm, proj_hm, slope_rows, gq, gs, *([pool_k] * n_pg), *([pool_v] * n_pg))


def _to_token_major(x_hm, batch, t):
    s = x_hm.shape[0]
    return jnp.transpose(x_hm, (1, 0, 2)).reshape(batch, t, s, HEAD_DIM)


def _rms_glue(x, g):
    y = x * lax.rsqrt(jnp.mean(x * x, axis=-1, keepdims=True) + EPS)
    return y * g


def _rope_glue(x, pos):
    half = x.shape[-1] // 2
    inv = ROPE_THETA ** (-jnp.arange(half, dtype=F32) / half)
    ang = pos.astype(F32)[:, None] * inv
    ang = ang.reshape((pos.shape[0],) + (1,) * (x.ndim - 3) + (half,))
    cos, sin = jnp.cos(ang), jnp.sin(ang)
    x1, x2 = x[..., :half], x[..., half:]
    return jnp.concatenate([x1 * cos - x2 * sin, x1 * sin + x2 * cos], axis=-1)


def _tri(ck):
    j = lax.broadcasted_iota(jnp.int32, (ck, ck), 0)
    s = lax.broadcasted_iota(jnp.int32, (ck, ck), 1)
    return (j > s).astype(BF16)


def _sb_mixer(h, shape, w_in, w_out, pools, page_table):
    batch, t = shape
    d = h.shape[1]
    n_q = d // HEAD_DIM
    n_kv = (w_in.shape[1] - d) // (2 * HEAD_DIM)
    proj = _mm(h, w_in, out_dtype=F32, tm=1024, tn=512, out_head_major=True, name="sb_in")
    if pools is None:
        o_hm = _sb_prompt_attn(proj, _tri(min(KEY_CHUNK, t)), batch=batch, seq=t, n_q=n_q, n_kv=n_kv)
    else:
        pk, pv = (p.reshape(p.shape[0], p.shape[1] * n_kv, HEAD_DIM) for p in pools)
        o_hm = _sb_sample_attn(proj, pk, pv, page_table, _tri(2 * pools[0].shape[1]),
                               batch=batch, t_dec=t, n_q=n_q, n_kv=n_kv)
    o = _mm(o_hm, w_out, out_dtype=F32, tm=1024, tn=512, a_head_major=True, name="sb_out")
    k = _to_token_major(proj[n_q:n_q + n_kv], batch, t)
    v = _to_token_major(proj[n_q + n_kv:], batch, t)
    return o, k, v


def _mla_mixer(h, shape, w, pools, page_table):
    batch, t = shape
    m = batch * t
    w_in_a, w_in_pe, g_q_a, g_kv_a, g_k_pe, w_q_b, g_q_head, wk_abs, wv_abs, w_out = w
    n_heads, nope, rank = wk_abs.shape
    rope = w_in_pe.shape[1]
    q_rank = g_q_a.shape[0]
    dq = nope + rope
    past = 0 if page_table is None else page_table.shape[1] * pools[0].shape[1]
    q_pos = past + jnp.arange(t, dtype=jnp.int32)

    proj = _mm(h, w_in_a, out_dtype=F32, tm=1024, tn=512, name="mla_in")
    k_pe = _mm(h, w_in_pe, out_dtype=F32, tm=1024, tn=rope, name="mla_in_pe")
    q_a = _rms_glue(proj[:, :q_rank], g_q_a).astype(BF16)
    ckv = _rms_glue(proj[:, q_rank:], g_kv_a)
    kpe = _rope_glue(_rms_glue(k_pe, g_k_pe).reshape(batch, t, rope), q_pos).reshape(m, rope)
    q = _mm(q_a, w_q_b, out_dtype=F32, tm=1024, tn=512, name="mla_q_b").reshape(batch, t, n_heads, dq)
    q = _rms_glue(q, g_q_head)
    q_pe = _rope_glue(q[..., nope:], q_pos)
    q_nope_hm = jnp.transpose(q[..., :nope].reshape(m, n_heads, nope), (1, 0, 2)).astype(BF16)
    q_pe_hm = jnp.transpose(q_pe.reshape(m, n_heads, rope), (1, 0, 2))
    scale = dq ** -0.5
    if pools is None:
        q_lat = _bmm(q_nope_hm, wk_abs, out_dtype=BF16, tm=1024, name="mla_q_lat")
        ctx = _mla_prompt_attn(q_lat, q_pe_hm, ckv, kpe, batch=batch, seq=t, scale=scale)
    else:
        q_lat = _bmm(q_nope_hm, wk_abs, out_dtype=F32, tm=1024, name="mla_q_lat")
        page = pools[0].shape[1]
        kpe_new_t = jnp.pad(jnp.swapaxes(kpe.reshape(batch, t, rope), 1, 2), ((0, 0), (0, 0), (0, page - t)))
        ctx = _mla_sample_attn(q_lat, q_pe_hm, ckv, kpe_new_t, pools[0], jnp.swapaxes(pools[1], 1, 2),
                               page_table, batch=batch, t_dec=t, scale=scale)
    o_hm = _bmm(ctx, wv_abs, out_dtype=BF16, tm=1024, name="mla_ctx_v")
    o = _mm(o_hm, w_out, out_dtype=F32, tm=1024, tn=512, a_head_major=True, name="mla_out")
    return o, ckv.reshape(batch, t, rank), kpe.reshape(batch, t, rope)


def _diff_mixer(h, shape, layer, w, pools, page_table):
    batch, t = shape
    m = batch * t
    w_in, g_q, g_k, lam_q1, lam_k1, lam_q2, lam_k2, g_subln, w_out = w
    n_heads = w_out.shape[0] // (2 * HEAD_DIM)
    lam_init = 0.8 - 0.6 * math.exp(-0.3 * layer)
    ex = lambda a, c: jnp.exp(jnp.sum(a * c))
    lam = ex(lam_q1, lam_k1) - ex(lam_q2, lam_k2) + lam_init
    slopes = 2.0 ** (-8.0 * jnp.arange(1, n_heads + 1, dtype=F32) / n_heads)
    par = jnp.concatenate([lam[None], slopes]).astype(F32)
    gq = g_q.reshape(1, HEAD_DIM)
    gs = g_subln.reshape(1, 2 * HEAD_DIM)

    proj = _mm(h, w_in, out_dtype=F32, tm=1024, tn=512, out_head_major=True, name="diff_in")
    k_hm = _rms_glue(proj[2 * n_heads:2 * n_heads + 2], g_k)
    if pools is None:
        o_hm = _diff_prompt_attn(proj, k_hm, par, gq, gs, batch=batch, seq=t, n_heads=n_heads,
                                 out_scale=1.0 - lam_init)
    else:
        page = pools[0].shape[1]
        pk, pv = (p.reshape(p.shape[0], 2 * page, HEAD_DIM) for p in pools)
        slope_rows = jnp.repeat(slopes, t)[:, None]
        o_hm = _diff_sample_attn(proj, k_hm, pk, pv, page_table, par, slope_rows, gq, gs,
                                 batch=batch, t_dec=t, n_heads=n_heads, out_scale=1.0 - lam_init)
    o = _mm(o_hm, w_out, out_dtype=F32, tm=1024, tn=512, a_head_major=True, name="diff_out")
    k = _to_token_major(k_hm, batch, t).reshape(batch, t, 1, 2 * HEAD_DIM)
    v = _to_token_major(proj[2 * n_heads + 2:], batch, t).reshape(batch, t, 1, 2 * HEAD_DIM)
    return o, k, v


def kernel(x_prompt, x_sample, c_prompt, c_sample, cache_l0_k, cache_l0_v, cache_l1_ckv, cache_l1_kpe, cache_l2_k, cache_l2_v, cache_l3_k, cache_l3_v, page_table, w_ada, b_ada, ada_table, g_norm_mix, g_norm_ffn, w_in_sb, w_out_sb, w_in_mla, g_q_a, g_kv_a, g_k_pe, w_q_b, g_q_head, w_kv_b, w_out_mla, w_in_diff, g_q_diff, g_k_diff, lam_q1, lam_k1, lam_q2, lam_k2, g_subln, w_out_diff, w_gate_up, w_down):
    depth, n_chunks, d = ada_table.shape
    pools = ((cache_l0_k, cache_l0_v), (cache_l1_ckv, cache_l1_kpe),
             (cache_l2_k, cache_l2_v), (cache_l3_k, cache_l3_v))
    bf = lambda a: a.astype(BF16)

    n_p = c_prompt.shape[0]
    c_all = jnp.concatenate([c_prompt, c_sample], axis=0)
    silu_c = c_all * (1.0 / (1.0 + jnp.exp(-c_all)))
    mod_all = _mm(silu_c, bf(w_ada), out_dtype=F32, tm=c_all.shape[0], tn=512, name="ada") + b_ada
    mod_all = mod_all.reshape(c_all.shape[0], n_chunks, d)

    streams = [
        dict(x=x_prompt, mod=mod_all[:n_p], pools=None, bb=1, tt=256),
        dict(x=x_sample, mod=mod_all[n_p:], pools=pools, bb=32, tt=x_sample.shape[1]),
    ]

    q_rank, kv_rank, rope = g_q_a.shape[1], g_kv_a.shape[1], g_k_pe.shape[1]
    nope = g_q_head.shape[1] - rope

    new_rows = [[], []]
    pending = [None, None]
    for l in range(depth):
        kind, j = l % 3, l // 3
        w_gu, w_dn = bf(w_gate_up[l]), bf(w_down[l])
        if kind == 0:
            mix_w = (bf(w_in_sb[j]), bf(w_out_sb[j]))
        elif kind == 1:
            wkv = w_kv_b[j]
            mix_w = (bf(w_in_mla[j][:, :q_rank + kv_rank]), bf(w_in_mla[j][:, q_rank + kv_rank:]),
                     g_q_a[j], g_kv_a[j], g_k_pe[j], bf(w_q_b[j]), g_q_head[j],
                     bf(jnp.transpose(wkv[..., :nope], (1, 2, 0))),
                     bf(jnp.transpose(wkv[..., nope:], (1, 0, 2))),
                     bf(w_out_mla[j]))
        else:
            mix_w = (bf(w_in_diff[j]), g_q_diff[j], g_k_diff[j], lam_q1[j], lam_k1[j], lam_q2[j], lam_k2[j],
                     g_subln[j], bf(w_out_diff[j]))
        for si, st in enumerate(streams):
            x = st["x"]
            batch, t, _ = x.shape
            mod = st["mod"] + ada_table[l]
            chunk = lambda i: mod[:, i:i + 1, :]
            o_prev, gt_prev = pending[si] if pending[si] is not None else (None, None)
            x, h = _norm_stage(x, o_prev, gt_prev, g_norm_mix[l].reshape(1, 1, d), chunk(1), chunk(0),
                               bb=st["bb"], tt=st["tt"])
            tok_pools = None if st["pools"] is None else st["pools"][l]
            pt = None if st["pools"] is None else page_table
            if kind == 0:
                o, st_a, st_b = _sb_mixer(h, (batch, t), mix_w[0], mix_w[1], tok_pools, pt)
            elif kind == 1:
                o, st_a, st_b = _mla_mixer(h, (batch, t), mix_w, tok_pools, pt)
            else:
                o, st_a, st_b = _diff_mixer(h, (batch, t), l, mix_w, tok_pools, pt)
            new_rows[si] += [st_a, st_b]
            x, h = _norm_stage(x, o.reshape(batch, t, d), chunk(2), g_norm_ffn[l].reshape(1, 1, d),
                               chunk(4), chunk(3), bb=st["bb"], tt=st["tt"])
            o = _ffn(h, w_gu, w_dn, tm=512, tf=256)
            pending[si] = (o.reshape(batch, t, d), chunk(5))
            st["x"] = x

    outs = []
    for si, st in enumerate(streams):
        o, gt = pending[si]
        y, _ = _norm_stage(st["x"], o, gt, None, None, None, bb=st["bb"], tt=st["tt"])
        outs.append(y)
    return (outs[0], outs[1], *new_rows[0], *new_rows[1])
```

```python
import functools
import math

import jax
import jax.numpy as jnp
from jax import lax
from jax.experimental import pallas as pl
from jax.experimental.pallas import tpu as pltpu

F32 = jnp.float32
BF16 = jnp.bfloat16

HEAD_DIM = 128
EPS = 1e-6
ROPE_THETA = 10000.0
NEG_BIG = -1e30
SB_DEAD_LOG = -104.0
VMEM_LIMIT_BYTES = 56 * 1024 * 1024
Q_BLOCK = 256
KEY_CHUNK = 256
PAGES_PER_STEP = 16
PAGES_PER_UPDATE = 8


def _params(*sem):
    return pltpu.CompilerParams(dimension_semantics=sem, vmem_limit_bytes=VMEM_LIMIT_BYTES)


def _tile(n, want, align):
    if n <= want:
        return n
    t = (want // align) * align
    while t > 0 and n % t:
        t -= align
    assert t > 0, (n, want, align)
    return t


def _dot(a, b):
    return jnp.dot(a, b, preferred_element_type=F32)


def _dot_t(a, b):
    return lax.dot_general(a, b, (((1,), (1,)), ((), ())), preferred_element_type=F32)


def _mm_kernel(a_ref, w_ref, o_ref, *, a_slices, o_slices):
    if a_slices:
        a = jnp.concatenate([a_ref[h].astype(BF16) for h in range(a_slices)], axis=-1)
    else:
        a = a_ref[...].astype(BF16)
    acc = _dot(a, w_ref[...])
    if o_slices:
        for h in range(o_slices):
            o_ref[h] = acc[:, h * HEAD_DIM:(h + 1) * HEAD_DIM].astype(o_ref.dtype)
    else:
        o_ref[...] = acc.astype(o_ref.dtype)


def _mm(a, w, *, out_dtype, tm, tn, a_head_major=False, out_head_major=False, name="mm"):
    if a_head_major:
        s_in, m, _ = a.shape
        k = s_in * HEAD_DIM
    else:
        m, k = a.shape
        s_in = 0
    n = w.shape[1]
    assert w.shape[0] == k
    tm, tn = _tile(m, tm, 8), _tile(n, tn, HEAD_DIM)
    if a_head_major:
        a_spec = pl.BlockSpec((s_in, tm, HEAD_DIM), lambda i, j: (0, i, 0))
    else:
        a_spec = pl.BlockSpec((tm, k), lambda i, j: (i, 0))
    w_spec = pl.BlockSpec((k, tn), lambda i, j: (0, j))
    if out_head_major:
        assert tn % HEAD_DIM == 0
        s_out = tn // HEAD_DIM
        out_shape = jax.ShapeDtypeStruct((n // HEAD_DIM, m, HEAD_DIM), out_dtype)
        o_spec = pl.BlockSpec((s_out, tm, HEAD_DIM), lambda i, j: (j, i, 0))
    else:
        s_out = 0
        out_shape = jax.ShapeDtypeStruct((m, n), out_dtype)
        o_spec = pl.BlockSpec((tm, tn), lambda i, j: (i, j))
    return pl.pallas_call(
        functools.partial(_mm_kernel, a_slices=s_in, o_slices=s_out),
        grid=(m // tm, n // tn),
        in_specs=[a_spec, w_spec],
        out_specs=o_spec,
        out_shape=out_shape,
        compiler_params=_params("parallel", "arbitrary"),
        name=name,
    )(a, w)


def _bmm_kernel(a_ref, w_ref, o_ref):
    o_ref[...] = _dot(a_ref[...].astype(BF16), w_ref[...]).astype(o_ref.dtype)


def _bmm(a, w, *, out_dtype, tm, name="bmm"):
    h, m, k = a.shape
    n = w.shape[2]
    tm = _tile(m, tm, 8)
    return pl.pallas_call(
        _bmm_kernel,
        grid=(h, m // tm),
        in_specs=[pl.BlockSpec((None, tm, k), lambda g, i: (g, i, 0)),
                  pl.BlockSpec((None, k, n), lambda g, i: (g, 0, 0))],
        out_specs=pl.BlockSpec((None, tm, n), lambda g, i: (g, i, 0)),
        out_shape=jax.ShapeDtypeStruct((h, m, n), out_dtype),
        compiler_params=_params("parallel", "arbitrary"),
        name=name,
    )(a, w)


def _ffn_kernel(h_ref, wg_ref, wu_ref, wd_ref, o_ref):
    j = pl.program_id(1)
    h = h_ref[...]
    g = _dot(h, wg_ref[...])
    u = _dot(h, wu_ref[...])
    act = (g * (1.0 / (1.0 + jnp.exp(-g)))) * u
    part = _dot(act.astype(BF16), wd_ref[...])

    @pl.when(j == 0)
    def _():
        o_ref[...] = part

    @pl.when(j > 0)
    def _():
        o_ref[...] += part


def _ffn(h, w_gu, w_dn, *, tm, tf):
    m, d = h.shape
    d_ff = w_dn.shape[0]
    tm = _tile(m, tm, 8)
    assert d_ff % tf == 0
    nf = d_ff // tf
    return pl.pallas_call(
        _ffn_kernel,
        grid=(m // tm, nf),
        in_specs=[pl.BlockSpec((tm, d), lambda i, j: (i, 0)),
                  pl.BlockSpec((d, tf), lambda i, j: (0, j)),
                  pl.BlockSpec((d, tf), lambda i, j: (0, nf + j)),
                  pl.BlockSpec((tf, d), lambda i, j: (j, 0))],
        out_specs=pl.BlockSpec((tm, d), lambda i, j: (i, 0)),
        out_shape=jax.ShapeDtypeStruct((m, d), F32),
        compiler_params=_params("parallel", "arbitrary"),
        name="ffn",
    )(h, w_gu, w_gu, w_dn)


def _norm_kernel(*refs, has_resid, do_norm):
    refs = list(refs)
    x_ref = refs.pop(0)
    x = x_ref[...]
    if has_resid:
        o_ref = refs.pop(0)
        gt_ref = refs.pop(0)
        x = x + gt_ref[...] * o_ref[...]
    if do_norm:
        g_ref, sc_ref, sh_ref, xo_ref, h_ref = refs
    else:
        (xo_ref,) = refs
    xo_ref[...] = x
    if do_norm:
        ms = jnp.mean(x * x, axis=-1, keepdims=True)
        y = (x * lax.rsqrt(ms + EPS)) * g_ref[...]
        hm = y * (1.0 + sc_ref[...]) + sh_ref[...]
        bb, tt, d = hm.shape
        h_ref[...] = hm.reshape(bb * tt, d).astype(BF16)


def _norm_stage(x, o, gt, g, sc, sh, *, bb, tt):
    b, t, d = x.shape
    bb, tt = min(bb, b), min(tt, t)
    assert b % bb == 0 and t % tt == 0
    nt = t // tt
    has_resid = o is not None
    do_norm = g is not None
    big = pl.BlockSpec((bb, tt, d), lambda i, j: (i, j, 0))
    per_seq = pl.BlockSpec((bb, 1, d), lambda i, j: (i, 0, 0))
    args, specs = [x], [big]
    if has_resid:
        args += [o, gt]
        specs += [big, per_seq]
    out_shape = [jax.ShapeDtypeStruct((b, t, d), F32)]
    out_specs = [big]
    if do_norm:
        args += [g, sc, sh]
        specs += [pl.BlockSpec((1, 1, d), lambda i, j: (0, 0, 0)), per_seq, per_seq]
        out_shape.append(jax.ShapeDtypeStruct((b * t, d), BF16))
        out_specs.append(pl.BlockSpec((bb * tt, d), lambda i, j: (i * nt + j, 0)))
    res = pl.pallas_call(
        functools.partial(_norm_kernel, has_resid=has_resid, do_norm=do_norm),
        grid=(b // bb, nt),
        in_specs=specs,
        out_specs=out_specs,
        out_shape=out_shape,
        compiler_params=_params("parallel", "parallel"),
        name="norm_stage",
    )(*args)
    return (res[0], res[1]) if do_norm else (res[0], None)


def _sb_update(q, k, v, tri, strict, acc, carry):
    ck = tri.shape[0]
    n = k.shape[0] // ck
    z = _dot_t(q, k) * (HEAD_DIM ** -0.5)
    lsig = jnp.minimum(z, 0.0) - jnp.log(1.0 + jnp.exp(-jnp.abs(z)))
    ls = lsig - z
    if strict is not None:
        ls = jnp.where(strict, ls, 0.0)
    hi = ls.astype(BF16)
    lo = (ls - hi.astype(F32)).astype(BF16)
    la = [None] * n
    for c in range(n - 1, -1, -1):
        sl = slice(c * ck, (c + 1) * ck)
        la[c] = _dot(hi[:, sl], tri) + _dot(lo[:, sl], tri) + carry
        carry = carry + jnp.sum(ls[:, sl], axis=-1, keepdims=True)
    la = la[0] if n == 1 else jnp.concatenate(la, axis=1)
    a = jnp.exp(lsig + la)
    if strict is not None:
        a = jnp.where(strict, a, 0.0)
    acc = acc + _dot(a.astype(BF16), v)
    return acc, carry


def _softmax_update(s, v, m, l, acc):
    m_new = jnp.maximum(m, jnp.max(s, axis=-1, keepdims=True))
    alpha = jnp.exp(m - m_new)
    p = jnp.exp(s - m_new)
    l = alpha * l + jnp.sum(p, axis=-1, keepdims=True)
    acc = alpha * acc + _dot(p.astype(BF16), v)
    return m_new, l, acc


def _unroll(trip):
    return 2 if trip % 2 == 0 else 1


def _rms_rows(x, g):
    return (x * lax.rsqrt(jnp.mean(x * x, axis=-1, keepdims=True) + EPS)) * g


def _causal_pairs(n_blocks):
    qi, ci = [], []
    for q in range(n_blocks):
        for c in range(q, -1, -1):
            qi.append(q)
            ci.append(c)
    return jnp.asarray(qi, jnp.int32), jnp.asarray(ci, jnp.int32)


def _block_positions(qi, c, tq, ck):
    qpos = qi * tq + lax.broadcasted_iota(jnp.int32, (tq, 1), 0)
    kpos = c * ck + lax.broadcasted_iota(jnp.int32, (1, ck), 1)
    return qpos, kpos


def _sb_prompt_kernel(qi_ref, ci_ref, q_ref, k_ref, v_ref, tri_ref, o_ref, acc_ref, carry_ref,
                      *, tq, ck, group):
    p = pl.program_id(2)
    qi, c = qi_ref[p], ci_ref[p]

    @pl.when(c == qi)
    def _():
        acc_ref[...] = jnp.zeros_like(acc_ref)
        carry_ref[...] = jnp.zeros_like(carry_ref)

    k = k_ref[...].astype(BF16)
    v = v_ref[...].astype(BF16)
    tri = tri_ref[...]

    step = _unroll(group)

    def update(g, strict):
        acc, carry = _sb_update(q_ref[g].astype(BF16), k, v, tri, strict, acc_ref[g], carry_ref[g])
        acc_ref[g] = acc
        carry_ref[g] = carry

    @pl.when(c == qi)
    def _():
        qpos, kpos = _block_positions(qi, c, tq, ck)
        strict = kpos < qpos

        def heads(gp, _):
            for r in range(step):
                update(gp * step + r, strict)
            return 0

        lax.fori_loop(0, group // step, heads, 0)

    @pl.when(c < qi)
    def _():
        def heads(gp, _):
            live = jnp.max(carry_ref[pl.ds(gp * step, step)]) > SB_DEAD_LOG

            @pl.when(live)
            def _():
                for r in range(step):
                    update(gp * step + r, None)

            return 0

        lax.fori_loop(0, group // step, heads, 0)

    @pl.when(c == 0)
    def _():
        o_ref[...] = acc_ref[...].astype(o_ref.dtype)


def _sb_prompt_attn(proj_hm, tri, *, batch, seq, n_q, n_kv):
    tq = ck = min(Q_BLOCK, seq)
    assert seq % tq == 0
    nb = seq // tq
    group = n_q // n_kv
    qi_tab, ci_tab = _causal_pairs(nb)
    grid_spec = pltpu.PrefetchScalarGridSpec(
        num_scalar_prefetch=2,
        grid=(batch, n_kv, int(qi_tab.shape[0])),
        in_specs=[
            pl.BlockSpec((group, tq, HEAD_DIM), lambda b, h, p, qi, ci: (h, b * nb + qi[p], 0)),
            pl.BlockSpec((None, ck, HEAD_DIM), lambda b, h, p, qi, ci: (n_q + h, b * nb + ci[p], 0)),
            pl.BlockSpec((None, ck, HEAD_DIM), lambda b, h, p, qi, ci: (n_q + n_kv + h, b * nb + ci[p], 0)),
            pl.BlockSpec((ck, ck), lambda b, h, p, qi, ci: (0, 0)),
        ],
        out_specs=pl.BlockSpec((group, tq, HEAD_DIM), lambda b, h, p, qi, ci: (h, b * nb + qi[p], 0)),
        scratch_shapes=[pltpu.VMEM((group, tq, HEAD_DIM), F32), pltpu.VMEM((group, tq, 1), F32)],
    )
    return pl.pallas_call(
        functools.partial(_sb_prompt_kernel, tq=tq, ck=ck, group=group),
        grid_spec=grid_spec,
        out_shape=jax.ShapeDtypeStruct((n_q, batch * seq, HEAD_DIM), BF16),
        compiler_params=_params("parallel", "parallel", "arbitrary"),
        name="sb_prompt_attn",
    )(qi_tab, ci_tab, proj_hm, proj_hm, proj_hm, tri)


def _diff_prompt_kernel(qi_ref, ci_ref, par_ref, q_ref, k_ref, v_ref, gq_ref, gs_ref, o_ref,
                        m_ref, l_ref, acc_ref, *, tq, ck, group, head_base, out_scale):
    p = pl.program_id(2)
    hg = pl.program_id(1)
    qi, c = qi_ref[p], ci_ref[p]

    @pl.when(c == qi)
    def _():
        m_ref[...] = jnp.full_like(m_ref, NEG_BIG)
        l_ref[...] = jnp.zeros_like(l_ref)
        acc_ref[...] = jnp.zeros_like(acc_ref)

    qpos, kpos = _block_positions(qi, c, tq, ck)
    dist = (qpos - kpos).astype(F32)
    valid = dist >= 0.0
    v = jnp.concatenate([v_ref[0], v_ref[1]], axis=-1).astype(BF16)
    ks = (k_ref[0].astype(BF16), k_ref[1].astype(BF16))
    gq = gq_ref[...]

    def head(g, _):
        slope = par_ref[head_base + hg * group + g]
        bias = slope * dist
        for i in range(2):
            qn = _rms_rows(q_ref[2 * g + i], gq).astype(BF16)
            s = _dot_t(qn, ks[i]) * (HEAD_DIM ** -0.5) - bias
            s = jnp.where(valid, s, -jnp.inf)
            m, l, acc = _softmax_update(s, v, m_ref[i, g], l_ref[i, g], acc_ref[i, g])
            m_ref[i, g] = m
            l_ref[i, g] = l
            acc_ref[i, g] = acc
        return 0

    lax.fori_loop(0, group, head, 0)

    @pl.when(c == 0)
    def _():
        lam = par_ref[0]
        gs = gs_ref[...]

        def fin(g, _):
            o = acc_ref[0, g] / l_ref[0, g] - lam * (acc_ref[1, g] / l_ref[1, g])
            o = _rms_rows(o, gs) * out_scale
            o_ref[2 * g] = o[:, :HEAD_DIM].astype(o_ref.dtype)
            o_ref[2 * g + 1] = o[:, HEAD_DIM:].astype(o_ref.dtype)
            return 0

        lax.fori_loop(0, group, fin, 0)


def _diff_prompt_attn(proj_hm, k_hm, par, gq, gs, *, batch, seq, n_heads, out_scale):
    tq = ck = min(Q_BLOCK, seq)
    assert seq % tq == 0
    nb = seq // tq
    n_groups = 2 if n_heads % 2 == 0 else 1
    group = n_heads // n_groups
    qi_tab, ci_tab = _causal_pairs(nb)
    v_blk = (2 * n_heads + 2) // 2
    grid_spec = pltpu.PrefetchScalarGridSpec(
        num_scalar_prefetch=2,
        grid=(batch, n_groups, int(qi_tab.shape[0])),
        in_specs=[
            pl.BlockSpec(memory_space=pltpu.SMEM),
            pl.BlockSpec((2 * group, tq, HEAD_DIM), lambda b, h, p, qi, ci: (h, b * nb + qi[p], 0)),
            pl.BlockSpec((2, ck, HEAD_DIM), lambda b, h, p, qi, ci: (0, b * nb + ci[p], 0)),
            pl.BlockSpec((2, ck, HEAD_DIM), lambda b, h, p, qi, ci: (v_blk, b * nb + ci[p], 0)),
            pl.BlockSpec((1, HEAD_DIM), lambda b, h, p, qi, ci: (0, 0)),
            pl.BlockSpec((1, 2 * HEAD_DIM), lambda b, h, p, qi, ci: (0, 0)),
        ],
        out_specs=pl.BlockSpec((2 * group, tq, HEAD_DIM), lambda b, h, p, qi, ci: (h, b * nb + qi[p], 0)),
        scratch_shapes=[pltpu.VMEM((2, group, tq, 1), F32), pltpu.VMEM((2, group, tq, 1), F32),
                        pltpu.VMEM((2, group, tq, 2 * HEAD_DIM), F32)],
    )
    return pl.pallas_call(
        functools.partial(_diff_prompt_kernel, tq=tq, ck=ck, group=group, head_base=1,
                          out_scale=out_scale),
        grid_spec=grid_spec,
        out_shape=jax.ShapeDtypeStruct((2 * n_heads, batch * seq, HEAD_DIM), BF16),
        compiler_params=_params("parallel", "parallel", "arbitrary"),
        name="diff_prompt_attn",
    )(qi_tab, ci_tab, par, proj_hm, k_hm, proj_hm, gq, gs)


def _mla_prompt_kernel(qi_ref, ci_ref, qn_ref, qp_ref, kv_ref, kpe_ref, o_ref, m_ref, l_ref, acc_ref,
                       *, tq, ck, group, scale):
    p = pl.program_id(2)
    qi, c = qi_ref[p], ci_ref[p]

    @pl.when(c == qi)
    def _():
        m_ref[...] = jnp.full_like(m_ref, NEG_BIG)
        l_ref[...] = jnp.zeros_like(l_ref)
        acc_ref[...] = jnp.zeros_like(acc_ref)

    kpe = kpe_ref[...].astype(BF16)

    def sweep(valid):
        def head(g, _):
            s = (_dot_t(qn_ref[g], kv_ref[2 * g]) + _dot_t(qp_ref[g], kpe)) * scale
            if valid is not None:
                s = jnp.where(valid, s, -jnp.inf)
            m, l, acc = _softmax_update(s, kv_ref[2 * g + 1], m_ref[g], l_ref[g], acc_ref[g])
            m_ref[g] = m
            l_ref[g] = l
            acc_ref[g] = acc
            return 0

        lax.fori_loop(0, group, head, 0, unroll=_unroll(group))

    @pl.when(c == qi)
    def _():
        qpos, kpos = _block_positions(qi, c, tq, ck)
        sweep(kpos <= qpos)

    @pl.when(c < qi)
    def _():
        sweep(None)

    @pl.when(c == 0)
    def _():
        o_ref[...] = (acc_ref[...] / l_ref[...]).astype(o_ref.dtype)


def _mla_prompt_attn(q_nope, q_pe, kv_hm, kpe, *, batch, seq, scale):
    n_heads, _, nope = q_nope.shape
    rope = q_pe.shape[2]
    tq = ck = min(Q_BLOCK, seq)
    assert seq % tq == 0
    nb = seq // tq
    n_groups = 2 if n_heads % 2 == 0 else 1
    group = n_heads // n_groups
    qi_tab, ci_tab = _causal_pairs(nb)
    grid_spec = pltpu.PrefetchScalarGridSpec(
        num_scalar_prefetch=2,
        grid=(batch, n_groups, int(qi_tab.shape[0])),
        in_specs=[
            pl.BlockSpec((group, tq, nope), lambda b, h, p, qi, ci: (h, b * nb + qi[p], 0)),
            pl.BlockSpec((group, tq, rope), lambda b, h, p, qi, ci: (h, b * nb + qi[p], 0)),
            pl.BlockSpec((2 * group, ck, HEAD_DIM), lambda b, h, p, qi, ci: (h, b * nb + ci[p], 0)),
            pl.BlockSpec((ck, rope), lambda b, h, p, qi, ci: (b * nb + ci[p], 0)),
        ],
        out_specs=pl.BlockSpec((group, tq, HEAD_DIM), lambda b, h, p, qi, ci: (h, b * nb + qi[p], 0)),
        scratch_shapes=[pltpu.VMEM((group, tq, 1), F32), pltpu.VMEM((group, tq, 1), F32),
                        pltpu.VMEM((group, tq, HEAD_DIM), F32)],
    )
    return pl.pallas_call(
        functools.partial(_mla_prompt_kernel, tq=tq, ck=ck, group=group, scale=scale),
        grid_spec=grid_spec,
        out_shape=jax.ShapeDtypeStruct((n_heads, batch * seq, HEAD_DIM), BF16),
        compiler_params=_params("parallel", "parallel", "arbitrary"),
        name="mla_prompt_attn",
    )(qi_tab, ci_tab, q_nope, q_pe, kv_hm, kpe)


def _pages_per_step(n_pages):
    p = min(PAGES_PER_STEP, n_pages)
    while n_pages % p or p % 2:
        p -= 1
    assert p >= 2, "page count must be even"
    return p


def _pages_per_update(n_pg):
    w = min(PAGES_PER_UPDATE, n_pg)
    while n_pg % w or w % 2:
        w -= 1
    return w


def _page_specs(block, n, step_page0):
    trail = (0,) * (len(block) - 1)
    return [pl.BlockSpec(block, (lambda b, s, pt, r=r: (pt[b, step_page0(s) + r],) + trail))
            for r in range(n)]


def _sb_sample_kernel(pt_ref, q_ref, kn_ref, vn_ref, tri_ref, *refs, n_pg, n_kv, group, t_dec, n_steps):
    kp, vp = refs[:n_pg], refs[n_pg:2 * n_pg]
    o_ref, acc_ref, carry_ref, kbuf, vbuf = refs[2 * n_pg:]
    s = pl.program_id(1)
    rows = group * t_dec
    page = kp[0].shape[0] // n_kv
    tri = tri_ref[...]

    def q_rows(h):
        return q_ref[h * group:(h + 1) * group].reshape(rows, HEAD_DIM).astype(BF16)

    @pl.when(s == 0)
    def _():
        t_row = lax.rem(lax.broadcasted_iota(jnp.int32, (rows, 1), 0), t_dec)
        col = lax.broadcasted_iota(jnp.int32, (1, kbuf.shape[0]), 1)
        strict = col < t_row
        for h in range(n_kv):
            kbuf[...] = jnp.zeros_like(kbuf)
            vbuf[...] = jnp.zeros_like(vbuf)
            kbuf[0:t_dec] = kn_ref[h]
            vbuf[0:t_dec] = vn_ref[h]
            acc, carry = _sb_update(q_rows(h), kbuf[...].astype(BF16), vbuf[...].astype(BF16), tri, strict,
                                    jnp.zeros((rows, HEAD_DIM), F32), jnp.zeros((rows, 1), F32))
            acc_ref[h] = acc
            carry_ref[h] = carry

    wide = _pages_per_update(n_pg)

    @pl.when(jnp.max(carry_ref[...]) > SB_DEAD_LOG)
    def _():
        qs = [q_rows(h) for h in range(n_kv)]
        accs = [acc_ref[h] for h in range(n_kv)]
        carries = [carry_ref[h] for h in range(n_kv)]
        for u in range(n_pg // wide - 1, -1, -1):
            for h in range(n_kv):
                sel = pl.ds(h, page, stride=n_kv)
                pages = range(u * wide, (u + 1) * wide)
                k = jnp.concatenate([kp[r][sel, :] for r in pages], axis=0).astype(BF16)
                v = jnp.concatenate([vp[r][sel, :] for r in pages], axis=0).astype(BF16)
                accs[h], carries[h] = _sb_update(qs[h], k, v, tri, None, accs[h], carries[h])
        for h in range(n_kv):
            acc_ref[h] = accs[h]
            carry_ref[h] = carries[h]

    @pl.when(s == n_steps - 1)
    def _():
        for h in range(n_kv):
            o_ref[h * group:(h + 1) * group] = acc_ref[h].reshape(group, t_dec, HEAD_DIM).astype(o_ref.dtype)


def _sb_sample_attn(proj_hm, pool_k, pool_v, page_table, tri, *, batch, t_dec, n_q, n_kv):
    n_pages = page_table.shape[1]
    n_pg = _pages_per_step(n_pages)
    n_steps = n_pages // n_pg
    group = n_q // n_kv
    rows = group * t_dec
    ck = tri.shape[0]
    pg_block = (None,) + pool_k.shape[1:]
    page0 = lambda s: (n_steps - 1 - s) * n_pg
    grid_spec = pltpu.PrefetchScalarGridSpec(
        num_scalar_prefetch=1,
        grid=(batch, n_steps),
        in_specs=[
            pl.BlockSpec((n_q, t_dec, HEAD_DIM), lambda b, s, pt: (0, b, 0)),
            pl.BlockSpec((n_kv, t_dec, HEAD_DIM), lambda b, s, pt: (n_q // n_kv, b, 0)),
            pl.BlockSpec((n_kv, t_dec, HEAD_DIM), lambda b, s, pt: (n_q // n_kv + 1, b, 0)),
            pl.BlockSpec((ck, ck), lambda b, s, pt: (0, 0)),
        ] + _page_specs(pg_block, n_pg, page0) + _page_specs(pg_block, n_pg, page0),
        out_specs=pl.BlockSpec((n_q, t_dec, HEAD_DIM), lambda b, s, pt: (0, b, 0)),
        scratch_shapes=[pltpu.VMEM((n_kv, rows, HEAD_DIM), F32), pltpu.VMEM((n_kv, rows, 1), F32),
                        pltpu.VMEM((ck, HEAD_DIM), F32), pltpu.VMEM((ck, HEAD_DIM), F32)],
    )
    return pl.pallas_call(
        functools.partial(_sb_sample_kernel, n_pg=n_pg, n_kv=n_kv, group=group, t_dec=t_dec, n_steps=n_steps),
        grid_spec=grid_spec,
        out_shape=jax.ShapeDtypeStruct((n_q, batch * t_dec, HEAD_DIM), F32),
        compiler_params=_params("parallel", "arbitrary"),
        name="sb_sample_attn",
    )(page_table, proj_hm, proj_hm, proj_hm, tri, *([pool_k] * n_pg), *([pool_v] * n_pg))


def _mla_sample_kernel(pt_ref, ql_ref, qp_ref, cn_ref, knt_ref, *refs, n_pg, n_heads, t_dec, n_steps, scale):
    cp, kp = refs[:n_pg], refs[n_pg:2 * n_pg]
    o_ref, m_ref, l_ref, acc_ref, cbuf = refs[2 * n_pg:]
    s = pl.program_id(1)
    rows = n_heads * t_dec
    ql = ql_ref[...].reshape(rows, ql_ref.shape[2]).astype(BF16)
    qp = qp_ref[...].reshape(rows, qp_ref.shape[2]).astype(BF16)

    @pl.when(s == 0)
    def _():
        m_ref[...] = jnp.full_like(m_ref, NEG_BIG)
        l_ref[...] = jnp.zeros_like(l_ref)
        acc_ref[...] = jnp.zeros_like(acc_ref)

    def update(ckv, kpe_t, valid):
        sc = (_dot_t(ql, ckv) + _dot(qp, kpe_t)) * scale
        if valid is not None:
            sc = jnp.where(valid, sc, -jnp.inf)
        m, l, acc = _softmax_update(sc, ckv, m_ref[...], l_ref[...], acc_ref[...])
        m_ref[...] = m
        l_ref[...] = l
        acc_ref[...] = acc

    wide = _pages_per_update(n_pg)
    for u in range(n_pg // wide):
        pages = range(u * wide, (u + 1) * wide)
        ckv = jnp.concatenate([cp[r][...] for r in pages], axis=0).astype(BF16)
        kpe_t = jnp.concatenate([kp[r][...] for r in pages], axis=1).astype(BF16)
        update(ckv, kpe_t, None)

    @pl.when(s == n_steps - 1)
    def _():
        t_row = lax.rem(lax.broadcasted_iota(jnp.int32, (rows, 1), 0), t_dec)
        col = lax.broadcasted_iota(jnp.int32, (1, cbuf.shape[0]), 1)
        valid = col <= t_row
        cbuf[...] = jnp.zeros_like(cbuf)
        cbuf[0:t_dec] = cn_ref[...]
        update(cbuf[...].astype(BF16), knt_ref[...].astype(BF16), valid)
        o = acc_ref[...] / l_ref[...]
        o_ref[...] = o.reshape(n_heads, t_dec, o.shape[1]).astype(o_ref.dtype)


def _mla_sample_attn(q_lat, q_pe, ckv_new, kpe_new_t, pool_ckv, pool_kpe_t, page_table, *, batch, t_dec, scale):
    n_heads, _, rank = q_lat.shape
    rope = q_pe.shape[2]
    n_pages = page_table.shape[1]
    page = pool_ckv.shape[1]
    n_pg = _pages_per_step(n_pages)
    n_steps = n_pages // n_pg
    rows = n_heads * t_dec
    page0 = lambda s: s * n_pg
    grid_spec = pltpu.PrefetchScalarGridSpec(
        num_scalar_prefetch=1,
        grid=(batch, n_steps),
        in_specs=[
            pl.BlockSpec((n_heads, t_dec, rank), lambda b, s, pt: (0, b, 0)),
            pl.BlockSpec((n_heads, t_dec, rope), lambda b, s, pt: (0, b, 0)),
            pl.BlockSpec((t_dec, rank), lambda b, s, pt: (b, 0)),
            pl.BlockSpec((None, rope, page), lambda b, s, pt: (b, 0, 0)),
        ] + _page_specs((None, page, rank), n_pg, page0) + _page_specs((None, rope, page), n_pg, page0),
        out_specs=pl.BlockSpec((n_heads, t_dec, rank), lambda b, s, pt: (0, b, 0)),
        scratch_shapes=[pltpu.VMEM((rows, 1), F32), pltpu.VMEM((rows, 1), F32), pltpu.VMEM((rows, rank), F32),
                        pltpu.VMEM((page, rank), F32)],
    )
    return pl.pallas_call(
        functools.partial(_mla_sample_kernel, n_pg=n_pg, n_heads=n_heads, t_dec=t_dec, n_steps=n_steps,
                          scale=scale),
        grid_spec=grid_spec,
        out_shape=jax.ShapeDtypeStruct((n_heads, batch * t_dec, rank), F32),
        compiler_params=_params("parallel", "arbitrary"),
        name="mla_sample_attn",
    )(page_table, q_lat, q_pe, ckv_new, kpe_new_t, *([pool_ckv] * n_pg), *([pool_kpe_t] * n_pg))


def _diff_sample_kernel(pt_ref, par_ref, q_ref, kn_ref, vn_ref, slope_ref, gq_ref, gs_ref, *refs,
                        n_pg, n_heads, t_dec, n_steps, past, out_scale):
    kp, vp = refs[:n_pg], refs[n_pg:2 * n_pg]
    o_ref, m_ref, l_ref, acc_ref, kbuf, vbuf = refs[2 * n_pg:]
    s = pl.program_id(1)
    rows = n_heads * t_dec
    page = kp[0].shape[0] // 2
    ck = 2 * page
    gq = gq_ref[...]
    q4 = q_ref[...].reshape(n_heads, 2, t_dec, HEAD_DIM)
    qs = [_rms_rows(q4[:, i].reshape(rows, HEAD_DIM), gq).astype(BF16) for i in range(2)]
    slope = slope_ref[...]
    t_row = lax.rem(lax.broadcasted_iota(jnp.int32, (rows, 1), 0), t_dec)
    col = lax.broadcasted_iota(jnp.int32, (1, ck), 1)

    @pl.when(s == 0)
    def _():
        m_ref[...] = jnp.full_like(m_ref, NEG_BIG)
        l_ref[...] = jnp.zeros_like(l_ref)
        acc_ref[...] = jnp.zeros_like(acc_ref)

    def update(k1, k2, v, dist, valid):
        bias = slope * dist
        for i, k in enumerate((k1, k2)):
            sc = _dot_t(qs[i], k) * (HEAD_DIM ** -0.5) - bias
            if valid is not None:
                sc = jnp.where(valid, sc, -jnp.inf)
            m, l, acc = _softmax_update(sc, v, m_ref[i], l_ref[i], acc_ref[i])
            m_ref[i] = m
            l_ref[i] = l
            acc_ref[i] = acc

    def halves(page_refs, i):
        sel = pl.ds(i, page, stride=2)
        return jnp.concatenate([r[sel, :] for r in page_refs], axis=0)

    wide = _pages_per_update(n_pg)
    col_w = lax.broadcasted_iota(jnp.int32, (1, wide * page), 1)
    for u in range(n_pg // wide):
        kpages = kp[u * wide:(u + 1) * wide]
        vpages = vp[u * wide:(u + 1) * wide]
        k1 = halves(kpages, 0).astype(BF16)
        k2 = halves(kpages, 1).astype(BF16)
        v = jnp.concatenate([halves(vpages, 0), halves(vpages, 1)], axis=-1).astype(BF16)
        kpos = (s * n_pg + u * wide) * page + col_w
        dist = (past + t_row - kpos).astype(F32)
        update(k1, k2, v, dist, None)

    @pl.when(s == n_steps - 1)
    def _():
        kbuf[...] = jnp.zeros_like(kbuf)
        vbuf[...] = jnp.zeros_like(vbuf)
        for i in range(2):
            kbuf[i, 0:t_dec] = kn_ref[i]
        vbuf[0:t_dec, 0:HEAD_DIM] = vn_ref[0]
        vbuf[0:t_dec, HEAD_DIM:] = vn_ref[1]
        dist = (t_row - col).astype(F32)
        update(kbuf[0].astype(BF16), kbuf[1].astype(BF16), vbuf[...].astype(BF16), dist, dist >= 0.0)
        lam = par_ref[0]
        o = acc_ref[0] / l_ref[0] - lam * (acc_ref[1] / l_ref[1])
        o = _rms_rows(o, gs_ref[...]) * out_scale
        o4 = jnp.stack([o[:, :HEAD_DIM].reshape(n_heads, t_dec, HEAD_DIM),
                        o[:, HEAD_DIM:].reshape(n_heads, t_dec, HEAD_DIM)], axis=1)
        o_ref[...] = o4.reshape(2 * n_heads, t_dec, HEAD_DIM).astype(o_ref.dtype)


def _diff_sample_attn(proj_hm, k_hm, pool_k, pool_v, page_table, par, slope_rows, gq, gs,
                      *, batch, t_dec, n_heads, out_scale):
    n_pages = page_table.shape[1]
    page = pool_k.shape[1] // 2
    n_pg = _pages_per_step(n_pages)
    n_steps = n_pages // n_pg
    rows = n_heads * t_dec
    ck = 2 * page
    page0 = lambda s: s * n_pg
    pg_block = (None, 2 * page, HEAD_DIM)
    grid_spec = pltpu.PrefetchScalarGridSpec(
        num_scalar_prefetch=1,
        grid=(batch, n_steps),
        in_specs=[
            pl.BlockSpec(memory_space=pltpu.SMEM),
            pl.BlockSpec((2 * n_heads, t_dec, HEAD_DIM), lambda b, s, pt: (0, b, 0)),
            pl.BlockSpec((2, t_dec, HEAD_DIM), lambda b, s, pt: (0, b, 0)),
            pl.BlockSpec((2, t_dec, HEAD_DIM), lambda b, s, pt: (n_heads + 1, b, 0)),
            pl.BlockSpec((rows, 1), lambda b, s, pt: (0, 0)),
            pl.BlockSpec((1, HEAD_DIM), lambda b, s, pt: (0, 0)),
            pl.BlockSpec((1, 2 * HEAD_DIM), lambda b, s, pt: (0, 0)),
        ] + _page_specs(pg_block, n_pg, page0) + _page_specs(pg_block, n_pg, page0),
        out_specs=pl.BlockSpec((2 * n_heads, t_dec, HEAD_DIM), lambda b, s, pt: (0, b, 0)),
        scratch_shapes=[pltpu.VMEM((2, rows, 1), F32), pltpu.VMEM((2, rows, 1), F32),
                        pltpu.VMEM((2, rows, 2 * HEAD_DIM), F32),
                        pltpu.VMEM((2, ck, HEAD_DIM), F32), pltpu.VMEM((ck, 2 * HEAD_DIM), F32)],
    )
    return pl.pallas_call(
        functools.partial(_diff_sample_kernel, n_pg=n_pg, n_heads=n_heads, t_dec=t_dec, n_steps=n_steps,
                          past=n_pages * page, out_scale=out_scale),
        grid_spec=grid_spec,
        out_shape=jax.ShapeDtypeStruct((2 * n_heads, batch * t_dec, HEAD_DIM), F32),
        compiler_params=_params("parallel", "arbitrary"),
        name="diff_sample_attn",
    )(page_table, par, proj_hm, k_hm, proj_hm, slope_rows, gq, gs, *([pool_k] * n_pg), *([pool_v] * n_pg))


def _to_token_major(x_hm, batch, t):
    s = x_hm.shape[0]
    return jnp.transpose(x_hm, (1, 0, 2)).reshape(batch, t, s, HEAD_DIM)


def _rms_glue(x, g):
    y = x * lax.rsqrt(jnp.mean(x * x, axis=-1, keepdims=True) + EPS)
    return y * g


def _rope_glue(x, pos):
    half = x.shape[-1] // 2
    inv = ROPE_THETA ** (-jnp.arange(half, dtype=F32) / half)
    ang = pos.astype(F32)[:, None] * inv
    ang = ang.reshape((pos.shape[0],) + (1,) * (x.ndim - 3) + (half,))
    cos, sin = jnp.cos(ang), jnp.sin(ang)
    x1, x2 = x[..., :half], x[..., half:]
    return jnp.concatenate([x1 * cos - x2 * sin, x1 * sin + x2 * cos], axis=-1)


def _tri(ck):
    j = lax.broadcasted_iota(jnp.int32, (ck, ck), 0)
    s = lax.broadcasted_iota(jnp.int32, (ck, ck), 1)
    return (j > s).astype(BF16)


def _sb_mixer(h, shape, w_in, w_out, pools, page_table):
    batch, t = shape
    d = h.shape[1]
    n_q = d // HEAD_DIM
    n_kv = (w_in.shape[1] - d) // (2 * HEAD_DIM)
    proj = _mm(h, w_in, out_dtype=F32, tm=1024, tn=512, out_head_major=True, name="sb_in")
    if pools is None:
        o_hm = _sb_prompt_attn(proj, _tri(min(KEY_CHUNK, t)), batch=batch, seq=t, n_q=n_q, n_kv=n_kv)
    else:
        pk, pv = (p.reshape(p.shape[0], p.shape[1] * n_kv, HEAD_DIM) for p in pools)
        o_hm = _sb_sample_attn(proj, pk, pv, page_table, _tri(2 * pools[0].shape[1]),
                               batch=batch, t_dec=t, n_q=n_q, n_kv=n_kv)
    o = _mm(o_hm, w_out, out_dtype=F32, tm=1024, tn=512, a_head_major=True, name="sb_out")
    k = _to_token_major(proj[n_q:n_q + n_kv], batch, t)
    v = _to_token_major(proj[n_q + n_kv:], batch, t)
    return o, k, v


def _mla_mixer(h, shape, w, pools, page_table):
    batch, t = shape
    m = batch * t
    w_in_a, w_in_pe, g_q_a, g_kv_a, g_k_pe, w_q_b, g_q_head, wk_abs, wv_abs, w_kv_flat, w_out = w
    n_heads, nope, rank = wk_abs.shape
    rope = w_in_pe.shape[1]
    q_rank = g_q_a.shape[0]
    dq = nope + rope
    past = 0 if page_table is None else page_table.shape[1] * pools[0].shape[1]
    q_pos = past + jnp.arange(t, dtype=jnp.int32)

    proj = _mm(h, w_in_a, out_dtype=F32, tm=1024, tn=512, name="mla_in")
    k_pe = _mm(h, w_in_pe, out_dtype=F32, tm=1024, tn=rope, name="mla_in_pe")
    q_a = _rms_glue(proj[:, :q_rank], g_q_a).astype(BF16)
    ckv = _rms_glue(proj[:, q_rank:], g_kv_a)
    kpe = _rope_glue(_rms_glue(k_pe, g_k_pe).reshape(batch, t, rope), q_pos).reshape(m, rope)
    q = _mm(q_a, w_q_b, out_dtype=F32, tm=1024, tn=512, name="mla_q_b").reshape(batch, t, n_heads, dq)
    q = _rms_glue(q, g_q_head)
    q_pe = _rope_glue(q[..., nope:], q_pos)
    q_nope_hm = jnp.transpose(q[..., :nope].reshape(m, n_heads, nope), (1, 0, 2)).astype(BF16)
    q_pe_hm = jnp.transpose(q_pe.reshape(m, n_heads, rope), (1, 0, 2))
    scale = dq ** -0.5
    if pools is None:
        kv_hm = _mm(ckv.astype(BF16), w_kv_flat, out_dtype=BF16, tm=1024, tn=512, out_head_major=True,
                    name="mla_kv_b")
        o_hm = _mla_prompt_attn(q_nope_hm, q_pe_hm.astype(BF16), kv_hm, kpe, batch=batch, seq=t, scale=scale)
    else:
        q_lat = _bmm(q_nope_hm, wk_abs, out_dtype=F32, tm=1024, name="mla_q_lat")
        page = pools[0].shape[1]
        kpe_new_t = jnp.pad(jnp.swapaxes(kpe.reshape(batch, t, rope), 1, 2), ((0, 0), (0, 0), (0, page - t)))
        ctx = _mla_sample_attn(q_lat, q_pe_hm, ckv, kpe_new_t, pools[0], jnp.swapaxes(pools[1], 1, 2),
                               page_table, batch=batch, t_dec=t, scale=scale)
        o_hm = _bmm(ctx, wv_abs, out_dtype=BF16, tm=1024, name="mla_ctx_v")
    o = _mm(o_hm, w_out, out_dtype=F32, tm=1024, tn=512, a_head_major=True, name="mla_out")
    return o, ckv.reshape(batch, t, rank), kpe.reshape(batch, t, rope)


def _diff_mixer(h, shape, layer, w, pools, page_table):
    batch, t = shape
    m = batch * t
    w_in, g_q, g_k, lam_q1, lam_k1, lam_q2, lam_k2, g_subln, w_out = w
    n_heads = w_out.shape[0] // (2 * HEAD_DIM)
    lam_init = 0.8 - 0.6 * math.exp(-0.3 * layer)
    ex = lambda a, c: jnp.exp(jnp.sum(a * c))
    lam = ex(lam_q1, lam_k1) - ex(lam_q2, lam_k2) + lam_init
    slopes = 2.0 ** (-8.0 * jnp.arange(1, n_heads + 1, dtype=F32) / n_heads)
    par = jnp.concatenate([lam[None], slopes]).astype(F32)
    gq = g_q.reshape(1, HEAD_DIM)
    gs = g_subln.reshape(1, 2 * HEAD_DIM)

    proj = _mm(h, w_in, out_dtype=F32, tm=1024, tn=512, out_head_major=True, name="diff_in")
    k_hm = _rms_glue(proj[2 * n_heads:2 * n_heads + 2], g_k)
    if pools is None:
        o_hm = _diff_prompt_attn(proj, k_hm, par, gq, gs, batch=batch, seq=t, n_heads=n_heads,
                                 out_scale=1.0 - lam_init)
    else:
        page = pools[0].shape[1]
        pk, pv = (p.reshape(p.shape[0], 2 * page, HEAD_DIM) for p in pools)
        slope_rows = jnp.repeat(slopes, t)[:, None]
        o_hm = _diff_sample_attn(proj, k_hm, pk, pv, page_table, par, slope_rows, gq, gs,
                                 batch=batch, t_dec=t, n_heads=n_heads, out_scale=1.0 - lam_init)
    o = _mm(o_hm, w_out, out_dtype=F32, tm=1024, tn=512, a_head_major=True, name="diff_out")
    k = _to_token_major(k_hm, batch, t).reshape(batch, t, 1, 2 * HEAD_DIM)
    v = _to_token_major(proj[2 * n_heads + 2:], batch, t).reshape(batch, t, 1, 2 * HEAD_DIM)
    return o, k, v


def kernel(x_prompt, x_sample, c_prompt, c_sample, cache_l0_k, cache_l0_v, cache_l1_ckv, cache_l1_kpe, cache_l2_k, cache_l2_v, cache_l3_k, cache_l3_v, page_table, w_ada, b_ada, ada_table, g_norm_mix, g_norm_ffn, w_in_sb, w_out_sb, w_in_mla, g_q_a, g_kv_a, g_k_pe, w_q_b, g_q_head, w_kv_b, w_out_mla, w_in_diff, g_q_diff, g_k_diff, lam_q1, lam_k1, lam_q2, lam_k2, g_subln, w_out_diff, w_gate_up, w_down):
    depth, n_chunks, d = ada_table.shape
    pools = ((cache_l0_k, cache_l0_v), (cache_l1_ckv, cache_l1_kpe),
             (cache_l2_k, cache_l2_v), (cache_l3_k, cache_l3_v))
    bf = lambda a: a.astype(BF16)

    n_p = c_prompt.shape[0]
    c_all = jnp.concatenate([c_prompt, c_sample], axis=0)
    silu_c = c_all * (1.0 / (1.0 + jnp.exp(-c_all)))
    mod_all = _mm(silu_c, bf(w_ada), out_dtype=F32, tm=c_all.shape[0], tn=512, name="ada") + b_ada
    mod_all = mod_all.reshape(c_all.shape[0], n_chunks, d)

    streams = [
        dict(x=x_prompt, mod=mod_all[:n_p], pools=None, bb=1, tt=256),
        dict(x=x_sample, mod=mod_all[n_p:], pools=pools, bb=32, tt=x_sample.shape[1]),
    ]

    q_rank, kv_rank, rope = g_q_a.shape[1], g_kv_a.shape[1], g_k_pe.shape[1]
    nope = g_q_head.shape[1] - rope

    new_rows = [[], []]
    pending = [None, None]
    for l in range(depth):
        kind, j = l % 3, l // 3
        w_gu, w_dn = bf(w_gate_up[l]), bf(w_down[l])
        if kind == 0:
            mix_w = (bf(w_in_sb[j]), bf(w_out_sb[j]))
        elif kind == 1:
            wkv = w_kv_b[j]
            mix_w = (bf(w_in_mla[j][:, :q_rank + kv_rank]), bf(w_in_mla[j][:, q_rank + kv_rank:]),
                     g_q_a[j], g_kv_a[j], g_k_pe[j], bf(w_q_b[j]), g_q_head[j],
                     bf(jnp.transpose(wkv[..., :nope], (1, 2, 0))),
                     bf(jnp.transpose(wkv[..., nope:], (1, 0, 2))),
                     bf(wkv.reshape(kv_rank, -1)),
                     bf(w_out_mla[j]))
        else:
            mix_w = (bf(w_in_diff[j]), g_q_diff[j], g_k_diff[j], lam_q1[j], lam_k1[j], lam_q2[j], lam_k2[j],
                     g_subln[j], bf(w_out_diff[j]))
        for si, st in enumerate(streams):
            x = st["x"]
            batch, t, _ = x.shape
            mod = st["mod"] + ada_table[l]
            chunk = lambda i: mod[:, i:i + 1, :]
            o_prev, gt_prev = pending[si] if pending[si] is not None else (None, None)
            x, h = _norm_stage(x, o_prev, gt_prev, g_norm_mix[l].reshape(1, 1, d), chunk(1), chunk(0),
                               bb=st["bb"], tt=st["tt"])
            tok_pools = None if st["pools"] is None else st["pools"][l]
            pt = None if st["pools"] is None else page_table
            if kind == 0:
                o, st_a, st_b = _sb_mixer(h, (batch, t), mix_w[0], mix_w[1], tok_pools, pt)
            elif kind == 1:
                o, st_a, st_b = _mla_mixer(h, (batch, t), mix_w, tok_pools, pt)
            else:
                o, st_a, st_b = _diff_mixer(h, (batch, t), l, mix_w, tok_pools, pt)
            new_rows[si] += [st_a, st_b]
            x, h = _norm_stage(x, o.reshape(batch, t, d), chunk(2), g_norm_ffn[l].reshape(1, 1, d),
                               chunk(4), chunk(3), bb=st["bb"], tt=st["tt"])
            o = _ffn(h, w_gu, w_dn, tm=512, tf=256)
            pending[si] = (o.reshape(batch, t, d), chunk(5))
            st["x"] = x

    outs = []
    for si, st in enumerate(streams):
        o, gt = pending[si]
        y, _ = _norm_stage(st["x"], o, gt, None, None, None, bb=st["bb"], tt=st["tt"])
        outs.append(y)
    return (outs[0], outs[1], *new_rows[0], *new_rows[1])
```

```python
import functools
import math

import jax
import jax.numpy as jnp
from jax import lax
from jax.experimental import pallas as pl
from jax.experimental.pallas import tpu as pltpu

F32 = jnp.float32
BF16 = jnp.bfloat16

HEAD_DIM = 128
EPS = 1e-6
ROPE_THETA = 10000.0
NEG_BIG = -1e30
SB_DEAD_LOG = -104.0
VMEM_LIMIT_BYTES = 56 * 1024 * 1024
Q_BLOCK = 256
SB_KEY_BLOCK = 512
SOFTMAX_BLOCK = 512
KEY_CHUNK = 256
HEADS_PER_ITER = 4
PAGES_PER_STEP = 16
SB_PAGES_PER_STEP = 64
PAGES_PER_UPDATE = 8


def _params(*sem):
    return pltpu.CompilerParams(dimension_semantics=sem, vmem_limit_bytes=VMEM_LIMIT_BYTES)


def _tile(n, want, align):
    if n <= want:
        return n
    t = (want // align) * align
    while t > 0 and n % t:
        t -= align
    assert t > 0, (n, want, align)
    return t


def _dot(a, b):
    return jnp.dot(a, b, preferred_element_type=F32)


def _dot_t(a, b):
    return lax.dot_general(a, b, (((1,), (1,)), ((), ())), preferred_element_type=F32)


def _mm_kernel(a_ref, w_ref, o_ref, *, a_slices, o_slices):
    if a_slices:
        a = jnp.concatenate([a_ref[h].astype(BF16) for h in range(a_slices)], axis=-1)
    else:
        a = a_ref[...].astype(BF16)
    acc = _dot(a, w_ref[...])
    if o_slices:
        for h in range(o_slices):
            o_ref[h] = acc[:, h * HEAD_DIM:(h + 1) * HEAD_DIM].astype(o_ref.dtype)
    else:
        o_ref[...] = acc.astype(o_ref.dtype)


def _mm(a, w, *, out_dtype, tm, tn, a_head_major=False, out_head_major=False, name="mm"):
    if a_head_major:
        s_in, m, _ = a.shape
        k = s_in * HEAD_DIM
    else:
        m, k = a.shape
        s_in = 0
    n = w.shape[1]
    assert w.shape[0] == k
    tm, tn = _tile(m, tm, 8), _tile(n, tn, HEAD_DIM)
    if a_head_major:
        a_spec = pl.BlockSpec((s_in, tm, HEAD_DIM), lambda i, j: (0, i, 0))
    else:
        a_spec = pl.BlockSpec((tm, k), lambda i, j: (i, 0))
    w_spec = pl.BlockSpec((k, tn), lambda i, j: (0, j))
    if out_head_major:
        assert tn % HEAD_DIM == 0
        s_out = tn // HEAD_DIM
        out_shape = jax.ShapeDtypeStruct((n // HEAD_DIM, m, HEAD_DIM), out_dtype)
        o_spec = pl.BlockSpec((s_out, tm, HEAD_DIM), lambda i, j: (j, i, 0))
    else:
        s_out = 0
        out_shape = jax.ShapeDtypeStruct((m, n), out_dtype)
        o_spec = pl.BlockSpec((tm, tn), lambda i, j: (i, j))
    return pl.pallas_call(
        functools.partial(_mm_kernel, a_slices=s_in, o_slices=s_out),
        grid=(m // tm, n // tn),
        in_specs=[a_spec, w_spec],
        out_specs=o_spec,
        out_shape=out_shape,
        compiler_params=_params("parallel", "arbitrary"),
        name=name,
    )(a, w)


def _bmm_kernel(a_ref, w_ref, o_ref):
    o_ref[...] = _dot(a_ref[...].astype(BF16), w_ref[...]).astype(o_ref.dtype)


def _bmm(a, w, *, out_dtype, tm, name="bmm"):
    h, m, k = a.shape
    n = w.shape[2]
    tm = _tile(m, tm, 8)
    return pl.pallas_call(
        _bmm_kernel,
        grid=(h, m // tm),
        in_specs=[pl.BlockSpec((None, tm, k), lambda g, i: (g, i, 0)),
                  pl.BlockSpec((None, k, n), lambda g, i: (g, 0, 0))],
        out_specs=pl.BlockSpec((None, tm, n), lambda g, i: (g, i, 0)),
        out_shape=jax.ShapeDtypeStruct((h, m, n), out_dtype),
        compiler_params=_params("parallel", "arbitrary"),
        name=name,
    )(a, w)


def _ffn_kernel(h_ref, wg_ref, wu_ref, wd_ref, o_ref):
    @pl.when(pl.program_id(1) == 0)
    def _():
        o_ref[...] = jnp.zeros_like(o_ref)

    h = h_ref[...]
    g = _dot(h, wg_ref[...])
    u = _dot(h, wu_ref[...])
    act = (g * (1.0 / (1.0 + jnp.exp(-g)))) * u
    o_ref[...] += _dot(act.astype(BF16), wd_ref[...])


def _ffn(h, w_gu, w_dn, *, tm, tf):
    m, d = h.shape
    d_ff = w_dn.shape[0]
    tm = _tile(m, tm, 8)
    assert d_ff % tf == 0
    nf = d_ff // tf
    return pl.pallas_call(
        _ffn_kernel,
        grid=(m // tm, nf),
        in_specs=[pl.BlockSpec((tm, d), lambda i, j: (i, 0)),
                  pl.BlockSpec((d, tf), lambda i, j: (0, j)),
                  pl.BlockSpec((d, tf), lambda i, j: (0, nf + j)),
                  pl.BlockSpec((tf, d), lambda i, j: (j, 0))],
        out_specs=pl.BlockSpec((tm, d), lambda i, j: (i, 0)),
        out_shape=jax.ShapeDtypeStruct((m, d), F32),
        compiler_params=_params("parallel", "arbitrary"),
        name="ffn",
    )(h, w_gu, w_gu, w_dn)


def _norm_kernel(*refs, has_resid, do_norm):
    refs = list(refs)
    x_ref = refs.pop(0)
    x = x_ref[...]
    if has_resid:
        o_ref = refs.pop(0)
        gt_ref = refs.pop(0)
        x = x + gt_ref[...] * o_ref[...]
    if do_norm:
        g_ref, sc_ref, sh_ref, xo_ref, h_ref = refs
    else:
        (xo_ref,) = refs
    xo_ref[...] = x
    if do_norm:
        ms = jnp.mean(x * x, axis=-1, keepdims=True)
        y = (x * lax.rsqrt(ms + EPS)) * g_ref[...]
        hm = y * (1.0 + sc_ref[...]) + sh_ref[...]
        bb, tt, d = hm.shape
        h_ref[...] = hm.reshape(bb * tt, d).astype(BF16)


def _norm_stage(x, o, gt, g, sc, sh, *, bb, tt):
    b, t, d = x.shape
    bb, tt = min(bb, b), min(tt, t)
    assert b % bb == 0 and t % tt == 0
    nt = t // tt
    has_resid = o is not None
    do_norm = g is not None
    big = pl.BlockSpec((bb, tt, d), lambda i, j: (i, j, 0))
    per_seq = pl.BlockSpec((bb, 1, d), lambda i, j: (i, 0, 0))
    args, specs = [x], [big]
    if has_resid:
        args += [o, gt]
        specs += [big, per_seq]
    out_shape = [jax.ShapeDtypeStruct((b, t, d), F32)]
    out_specs = [big]
    if do_norm:
        args += [g, sc, sh]
        specs += [pl.BlockSpec((1, 1, d), lambda i, j: (0, 0, 0)), per_seq, per_seq]
        out_shape.append(jax.ShapeDtypeStruct((b * t, d), BF16))
        out_specs.append(pl.BlockSpec((bb * tt, d), lambda i, j: (i * nt + j, 0)))
    res = pl.pallas_call(
        functools.partial(_norm_kernel, has_resid=has_resid, do_norm=do_norm),
        grid=(b // bb, nt),
        in_specs=specs,
        out_specs=out_specs,
        out_shape=out_shape,
        compiler_params=_params("parallel", "parallel"),
        name="norm_stage",
    )(*args)
    return (res[0], res[1]) if do_norm else (res[0], None)


def _sb_update(q, k, v, tri, strict, acc, carry, k_transposed=False):
    ck = tri.shape[0]
    n = v.shape[0] // ck
    z = (_dot(q, k) if k_transposed else _dot_t(q, k)) * (HEAD_DIM ** -0.5)
    lsig = jnp.minimum(z, 0.0) - jnp.log(1.0 + jnp.exp(-jnp.abs(z)))
    ls = lsig - z
    if strict is not None:
        ls = jnp.where(strict, ls, 0.0)
    hi = ls.astype(BF16)
    lo = (ls - hi.astype(F32)).astype(BF16)
    la = [None] * n
    for c in range(n - 1, -1, -1):
        sl = slice(c * ck, (c + 1) * ck)
        la[c] = _dot(hi[:, sl], tri) + _dot(lo[:, sl], tri) + carry
        carry = carry + jnp.sum(ls[:, sl], axis=-1, keepdims=True)
    la = la[0] if n == 1 else jnp.concatenate(la, axis=1)
    a = jnp.exp(lsig + la)
    if strict is not None:
        a = jnp.where(strict, a, 0.0)
    acc = acc + _dot(a.astype(BF16), v)
    return acc, carry


def _softmax_update(s, v, m, l, acc):
    m_new = jnp.maximum(m, jnp.max(s, axis=-1, keepdims=True))
    alpha = jnp.exp(m - m_new)
    p = jnp.exp(s - m_new)
    l = alpha * l + jnp.sum(p, axis=-1, keepdims=True)
    acc = alpha * acc + _dot(p.astype(BF16), v)
    return m_new, l, acc


def _unroll(trip):
    for n in (HEADS_PER_ITER, 2):
        if trip % n == 0:
            return n
    return 1


def _rms_rows(x, g):
    return (x * lax.rsqrt(jnp.mean(x * x, axis=-1, keepdims=True) + EPS)) * g


def _causal_pairs(n_blocks, tq=1, ck=1):
    qi, ci = [], []
    for q in range(n_blocks):
        for c in range((q * tq + tq - 1) // ck, -1, -1):
            qi.append(q)
            ci.append(c)
    return jnp.asarray(qi, jnp.int32), jnp.asarray(ci, jnp.int32)


def _block_positions(qi, c, tq, ck):
    qpos = qi * tq + lax.broadcasted_iota(jnp.int32, (tq, 1), 0)
    kpos = c * ck + lax.broadcasted_iota(jnp.int32, (1, ck), 1)
    return qpos, kpos


def _sb_prompt_kernel(qi_ref, ci_ref, q_ref, k_ref, v_ref, tri_ref, o_ref, acc_ref, carry_ref,
                      *, tq, ck, group):
    p = pl.program_id(2)
    qi, c = qi_ref[p], ci_ref[p]
    diag = (qi * tq + (tq - 1)) // ck

    @pl.when(c == diag)
    def _():
        acc_ref[...] = jnp.zeros_like(acc_ref)
        carry_ref[...] = jnp.zeros_like(carry_ref)

    k_t = k_ref[...].T.astype(BF16)
    v = v_ref[...].astype(BF16)
    tri = tri_ref[...]

    step = _unroll(group)

    def update(g, strict):
        acc, carry = _sb_update(q_ref[g].astype(BF16), k_t, v, tri, strict, acc_ref[g], carry_ref[g],
                                k_transposed=True)
        acc_ref[g] = acc
        carry_ref[g] = carry

    @pl.when(c == diag)
    def _():
        qpos, kpos = _block_positions(qi, c, tq, ck)
        strict = kpos < qpos

        def heads(gp, _):
            for r in range(step):
                update(gp * step + r, strict)
            return 0

        lax.fori_loop(0, group // step, heads, 0)

    @pl.when(c < diag)
    def _():
        def heads(gp, _):
            live = jnp.max(carry_ref[pl.ds(gp * step, step)]) > SB_DEAD_LOG

            @pl.when(live)
            def _():
                for r in range(step):
                    update(gp * step + r, None)

            return 0

        lax.fori_loop(0, group // step, heads, 0)

    @pl.when(c == 0)
    def _():
        o_ref[...] = acc_ref[...].astype(o_ref.dtype)


def _sb_prompt_attn(proj_hm, tri, *, batch, seq, n_q, n_kv):
    tq, ck = min(Q_BLOCK, seq), min(SB_KEY_BLOCK, seq)
    assert seq % tq == 0 and seq % ck == 0 and ck % tri.shape[0] == 0
    nb, nkb = seq // tq, seq // ck
    group = n_q // n_kv
    qi_tab, ci_tab = _causal_pairs(nb, tq, ck)
    grid_spec = pltpu.PrefetchScalarGridSpec(
        num_scalar_prefetch=2,
        grid=(batch, n_kv, int(qi_tab.shape[0])),
        in_specs=[
            pl.BlockSpec((group, tq, HEAD_DIM), lambda b, h, p, qi, ci: (h, b * nb + qi[p], 0)),
            pl.BlockSpec((None, ck, HEAD_DIM), lambda b, h, p, qi, ci: (n_q + h, b * nkb + ci[p], 0)),
            pl.BlockSpec((None, ck, HEAD_DIM), lambda b, h, p, qi, ci: (n_q + n_kv + h, b * nkb + ci[p], 0)),
            pl.BlockSpec(tri.shape, lambda b, h, p, qi, ci: (0, 0)),
        ],
        out_specs=pl.BlockSpec((group, tq, HEAD_DIM), lambda b, h, p, qi, ci: (h, b * nb + qi[p], 0)),
        scratch_shapes=[pltpu.VMEM((group, tq, HEAD_DIM), F32), pltpu.VMEM((group, tq, 1), F32)],
    )
    return pl.pallas_call(
        functools.partial(_sb_prompt_kernel, tq=tq, ck=ck, group=group),
        grid_spec=grid_spec,
        out_shape=jax.ShapeDtypeStruct((n_q, batch * seq, HEAD_DIM), BF16),
        compiler_params=_params("parallel", "parallel", "arbitrary"),
        name="sb_prompt_attn",
    )(qi_tab, ci_tab, proj_hm, proj_hm, proj_hm, tri)


def _diff_prompt_kernel(qi_ref, ci_ref, par_ref, q_ref, k_ref, v_ref, gq_ref, gs_ref, o_ref,
                        m_ref, l_ref, acc_ref, *, tq, ck, group, head_base, out_scale):
    p = pl.program_id(2)
    hg = pl.program_id(1)
    qi, c = qi_ref[p], ci_ref[p]

    @pl.when(c == qi)
    def _():
        m_ref[...] = jnp.full_like(m_ref, NEG_BIG)
        l_ref[...] = jnp.zeros_like(l_ref)
        acc_ref[...] = jnp.zeros_like(acc_ref)

    qpos, kpos = _block_positions(qi, c, tq, ck)
    dist = (qpos - kpos).astype(F32)
    v = jnp.concatenate([v_ref[0], v_ref[1]], axis=-1).astype(BF16)
    ks_t = (k_ref[0].T.astype(BF16), k_ref[1].T.astype(BF16))
    gq = gq_ref[...]

    def sweep(masked):
        def head(g, _):
            slope = par_ref[head_base + hg * group + g]
            bias = slope * dist
            for i in range(2):
                qn = _rms_rows(q_ref[2 * g + i], gq).astype(BF16)
                s = _dot(qn, ks_t[i]) * (HEAD_DIM ** -0.5) - bias
                if masked:
                    s = jnp.where(dist >= 0.0, s, -jnp.inf)
                m, l, acc = _softmax_update(s, v, m_ref[i, g], l_ref[i, g], acc_ref[i, g])
                m_ref[i, g] = m
                l_ref[i, g] = l
                acc_ref[i, g] = acc
            return 0

        lax.fori_loop(0, group, head, 0)

    @pl.when(c == qi)
    def _():
        sweep(True)

    @pl.when(c < qi)
    def _():
        sweep(False)

    @pl.when(c == 0)
    def _():
        lam = par_ref[0]
        gs = gs_ref[...]

        def fin(g, _):
            o = acc_ref[0, g] / l_ref[0, g] - lam * (acc_ref[1, g] / l_ref[1, g])
            o = _rms_rows(o, gs) * out_scale
            o_ref[2 * g] = o[:, :HEAD_DIM].astype(o_ref.dtype)
            o_ref[2 * g + 1] = o[:, HEAD_DIM:].astype(o_ref.dtype)
            return 0

        lax.fori_loop(0, group, fin, 0)


def _diff_prompt_attn(proj_hm, k_hm, par, gq, gs, *, batch, seq, n_heads, out_scale):
    tq = ck = min(SOFTMAX_BLOCK, seq)
    assert seq % tq == 0
    nb = seq // tq
    n_groups = 2 if n_heads % 2 == 0 else 1
    group = n_heads // n_groups
    qi_tab, ci_tab = _causal_pairs(nb)
    v_blk = (2 * n_heads + 2) // 2
    grid_spec = pltpu.PrefetchScalarGridSpec(
        num_scalar_prefetch=2,
        grid=(batch, n_groups, int(qi_tab.shape[0])),
        in_specs=[
            pl.BlockSpec(memory_space=pltpu.SMEM),
            pl.BlockSpec((2 * group, tq, HEAD_DIM), lambda b, h, p, qi, ci: (h, b * nb + qi[p], 0)),
            pl.BlockSpec((2, ck, HEAD_DIM), lambda b, h, p, qi, ci: (0, b * nb + ci[p], 0)),
            pl.BlockSpec((2, ck, HEAD_DIM), lambda b, h, p, qi, ci: (v_blk, b * nb + ci[p], 0)),
            pl.BlockSpec((1, HEAD_DIM), lambda b, h, p, qi, ci: (0, 0)),
            pl.BlockSpec((1, 2 * HEAD_DIM), lambda b, h, p, qi, ci: (0, 0)),
        ],
        out_specs=pl.BlockSpec((2 * group, tq, HEAD_DIM), lambda b, h, p, qi, ci: (h, b * nb + qi[p], 0)),
        scratch_shapes=[pltpu.VMEM((2, group, tq, 1), F32), pltpu.VMEM((2, group, tq, 1), F32),
                        pltpu.VMEM((2, group, tq, 2 * HEAD_DIM), F32)],
    )
    return pl.pallas_call(
        functools.partial(_diff_prompt_kernel, tq=tq, ck=ck, group=group, head_base=1,
                          out_scale=out_scale),
        grid_spec=grid_spec,
        out_shape=jax.ShapeDtypeStruct((2 * n_heads, batch * seq, HEAD_DIM), BF16),
        compiler_params=_params("parallel", "parallel", "arbitrary"),
        name="diff_prompt_attn",
    )(qi_tab, ci_tab, par, proj_hm, k_hm, proj_hm, gq, gs)


def _mla_prompt_kernel(qi_ref, ci_ref, qn_ref, qp_ref, kv_ref, kpe_ref, o_ref, m_ref, l_ref, acc_ref,
                       *, tq, ck, group, scale):
    p = pl.program_id(2)
    qi, c = qi_ref[p], ci_ref[p]

    @pl.when(c == qi)
    def _():
        m_ref[...] = jnp.full_like(m_ref, NEG_BIG)
        l_ref[...] = jnp.zeros_like(l_ref)
        acc_ref[...] = jnp.zeros_like(acc_ref)

    kpe = kpe_ref[...].astype(BF16)

    def sweep(valid):
        def head(g, _):
            s = (_dot_t(qn_ref[g], kv_ref[2 * g]) + _dot_t(qp_ref[g], kpe)) * scale
            if valid is not None:
                s = jnp.where(valid, s, -jnp.inf)
            m, l, acc = _softmax_update(s, kv_ref[2 * g + 1], m_ref[g], l_ref[g], acc_ref[g])
            m_ref[g] = m
            l_ref[g] = l
            acc_ref[g] = acc
            return 0

        lax.fori_loop(0, group, head, 0, unroll=_unroll(group))

    @pl.when(c == qi)
    def _():
        qpos, kpos = _block_positions(qi, c, tq, ck)
        sweep(kpos <= qpos)

    @pl.when(c < qi)
    def _():
        sweep(None)

    @pl.when(c == 0)
    def _():
        o_ref[...] = (acc_ref[...] / l_ref[...]).astype(o_ref.dtype)


def _mla_prompt_attn(q_nope, q_pe, kv_hm, kpe, *, batch, seq, scale):
    n_heads, _, nope = q_nope.shape
    rope = q_pe.shape[2]
    tq = ck = min(SOFTMAX_BLOCK, seq)
    assert seq % tq == 0
    nb = seq // tq
    n_groups = 2 if n_heads % 2 == 0 else 1
    group = n_heads // n_groups
    qi_tab, ci_tab = _causal_pairs(nb)
    grid_spec = pltpu.PrefetchScalarGridSpec(
        num_scalar_prefetch=2,
        grid=(batch, n_groups, int(qi_tab.shape[0])),
        in_specs=[
            pl.BlockSpec((group, tq, nope), lambda b, h, p, qi, ci: (h, b * nb + qi[p], 0)),
            pl.BlockSpec((group, tq, rope), lambda b, h, p, qi, ci: (h, b * nb + qi[p], 0)),
            pl.BlockSpec((2 * group, ck, HEAD_DIM), lambda b, h, p, qi, ci: (h, b * nb + ci[p], 0)),
            pl.BlockSpec((ck, rope), lambda b, h, p, qi, ci: (b * nb + ci[p], 0)),
        ],
        out_specs=pl.BlockSpec((group, tq, HEAD_DIM), lambda b, h, p, qi, ci: (h, b * nb + qi[p], 0)),
        scratch_shapes=[pltpu.VMEM((group, tq, 1), F32), pltpu.VMEM((group, tq, 1), F32),
                        pltpu.VMEM((group, tq, HEAD_DIM), F32)],
    )
    return pl.pallas_call(
        functools.partial(_mla_prompt_kernel, tq=tq, ck=ck, group=group, scale=scale),
        grid_spec=grid_spec,
        out_shape=jax.ShapeDtypeStruct((n_heads, batch * seq, HEAD_DIM), BF16),
        compiler_params=_params("parallel", "parallel", "arbitrary"),
        name="mla_prompt_attn",
    )(qi_tab, ci_tab, q_nope, q_pe, kv_hm, kpe)


def _pages_per_step(n_pages, limit=PAGES_PER_STEP):
    p = min(limit, n_pages)
    while n_pages % p or p % 2:
        p -= 1
    assert p >= 2, "page count must be even"
    return p


def _pages_per_update(n_pg):
    w = min(PAGES_PER_UPDATE, n_pg)
    while n_pg % w or w % 2:
        w -= 1
    return w


def _page_specs(block, n, step_page0):
    trail = (0,) * (len(block) - 1)
    return [pl.BlockSpec(block, (lambda b, s, pt, r=r: (pt[b, step_page0(s) + r],) + trail))
            for r in range(n)]


def _sb_sample_kernel(pt_ref, q_ref, kn_ref, vn_ref, tri_ref, *refs, n_pg, n_kv, group, t_dec, n_steps):
    kp, vp = refs[:n_pg], refs[n_pg:2 * n_pg]
    o_ref, acc_ref, carry_ref, kbuf, vbuf = refs[2 * n_pg:]
    s = pl.program_id(1)
    rows = group * t_dec
    page = kp[0].shape[0] // n_kv
    tri = tri_ref[...]

    def q_rows(h):
        return q_ref[h * group:(h + 1) * group].reshape(rows, HEAD_DIM).astype(BF16)

    @pl.when(s == 0)
    def _():
        t_row = lax.rem(lax.broadcasted_iota(jnp.int32, (rows, 1), 0), t_dec)
        col = lax.broadcasted_iota(jnp.int32, (1, kbuf.shape[0]), 1)
        strict = col < t_row
        for h in range(n_kv):
            kbuf[...] = jnp.zeros_like(kbuf)
            vbuf[...] = jnp.zeros_like(vbuf)
            kbuf[0:t_dec] = kn_ref[h]
            vbuf[0:t_dec] = vn_ref[h]
            acc, carry = _sb_update(q_rows(h), kbuf[...].astype(BF16), vbuf[...].astype(BF16), tri, strict,
                                    jnp.zeros((rows, HEAD_DIM), F32), jnp.zeros((rows, 1), F32))
            acc_ref[h] = acc
            carry_ref[h] = carry

    wide = _pages_per_update(n_pg)

    for u in range(n_pg // wide - 1, -1, -1):
        @pl.when(jnp.max(carry_ref[...]) > SB_DEAD_LOG)
        def _(u=u):
            pages = range(u * wide, (u + 1) * wide)
            for h in range(n_kv):
                sel = pl.ds(h, page, stride=n_kv)
                k = jnp.concatenate([kp[r][sel, :] for r in pages], axis=0).astype(BF16)
                v = jnp.concatenate([vp[r][sel, :] for r in pages], axis=0).astype(BF16)
                acc, carry = _sb_update(q_rows(h), k, v, tri, None, acc_ref[h], carry_ref[h])
                acc_ref[h] = acc
                carry_ref[h] = carry

    @pl.when(s == n_steps - 1)
    def _():
        for h in range(n_kv):
            o_ref[h * group:(h + 1) * group] = acc_ref[h].reshape(group, t_dec, HEAD_DIM).astype(o_ref.dtype)


def _sb_sample_attn(proj_hm, pool_k, pool_v, page_table, tri, *, batch, t_dec, n_q, n_kv):
    n_pages = page_table.shape[1]
    n_pg = _pages_per_step(n_pages, SB_PAGES_PER_STEP)
    n_steps = n_pages // n_pg
    group = n_q // n_kv
    rows = group * t_dec
    ck = tri.shape[0]
    pg_block = (None,) + pool_k.shape[1:]
    page0 = lambda s: (n_steps - 1 - s) * n_pg
    grid_spec = pltpu.PrefetchScalarGridSpec(
        num_scalar_prefetch=1,
        grid=(batch, n_steps),
        in_specs=[
            pl.BlockSpec((n_q, t_dec, HEAD_DIM), lambda b, s, pt: (0, b, 0)),
            pl.BlockSpec((n_kv, t_dec, HEAD_DIM), lambda b, s, pt: (n_q // n_kv, b, 0)),
            pl.BlockSpec((n_kv, t_dec, HEAD_DIM), lambda b, s, pt: (n_q // n_kv + 1, b, 0)),
            pl.BlockSpec((ck, ck), lambda b, s, pt: (0, 0)),
        ] + _page_specs(pg_block, n_pg, page0) + _page_specs(pg_block, n_pg, page0),
        out_specs=pl.BlockSpec((n_q, t_dec, HEAD_DIM), lambda b, s, pt: (0, b, 0)),
        scratch_shapes=[pltpu.VMEM((n_kv, rows, HEAD_DIM), F32), pltpu.VMEM((n_kv, rows, 1), F32),
                        pltpu.VMEM((ck, HEAD_DIM), F32), pltpu.VMEM((ck, HEAD_DIM), F32)],
    )
    return pl.pallas_call(
        functools.partial(_sb_sample_kernel, n_pg=n_pg, n_kv=n_kv, group=group, t_dec=t_dec, n_steps=n_steps),
        grid_spec=grid_spec,
        out_shape=jax.ShapeDtypeStruct((n_q, batch * t_dec, HEAD_DIM), F32),
        compiler_params=_params("parallel", "arbitrary"),
        name="sb_sample_attn",
    )(page_table, proj_hm, proj_hm, proj_hm, tri, *([pool_k] * n_pg), *([pool_v] * n_pg))


def _mla_sample_kernel(pt_ref, ql_ref, qp_ref, cn_ref, knt_ref, *refs, n_pg, n_heads, t_dec, n_steps, scale):
    cp, kp = refs[:n_pg], refs[n_pg:2 * n_pg]
    o_ref, m_ref, l_ref, acc_ref, cbuf = refs[2 * n_pg:]
    s = pl.program_id(1)
    rows = n_heads * t_dec
    ql = ql_ref[...].reshape(rows, ql_ref.shape[2]).astype(BF16)
    qp = qp_ref[...].reshape(rows, qp_ref.shape[2]).astype(BF16)

    @pl.when(s == 0)
    def _():
        m_ref[...] = jnp.full_like(m_ref, NEG_BIG)
        l_ref[...] = jnp.zeros_like(l_ref)
        acc_ref[...] = jnp.zeros_like(acc_ref)

    def update(ckv, kpe_t, valid):
        sc = (_dot_t(ql, ckv) + _dot(qp, kpe_t)) * scale
        if valid is not None:
            sc = jnp.where(valid, sc, -jnp.inf)
        m, l, acc = _softmax_update(sc, ckv, m_ref[...], l_ref[...], acc_ref[...])
        m_ref[...] = m
        l_ref[...] = l
        acc_ref[...] = acc

    wide = _pages_per_update(n_pg)
    for u in range(n_pg // wide):
        pages = range(u * wide, (u + 1) * wide)
        ckv = jnp.concatenate([cp[r][...] for r in pages], axis=0).astype(BF16)
        kpe_t = jnp.concatenate([kp[r][...] for r in pages], axis=1).astype(BF16)
        update(ckv, kpe_t, None)

    @pl.when(s == n_steps - 1)
    def _():
        t_row = lax.rem(lax.broadcasted_iota(jnp.int32, (rows, 1), 0), t_dec)
        col = lax.broadcasted_iota(jnp.int32, (1, cbuf.shape[0]), 1)
        valid = col <= t_row
        cbuf[...] = jnp.zeros_like(cbuf)
        cbuf[0:t_dec] = cn_ref[...]
        update(cbuf[...].astype(BF16), knt_ref[...].astype(BF16), valid)
        o = acc_ref[...] / l_ref[...]
        o_ref[...] = o.reshape(n_heads, t_dec, o.shape[1]).astype(o_ref.dtype)


def _mla_sample_attn(q_lat, q_pe, ckv_new, kpe_new_t, pool_ckv, pool_kpe_t, page_table, *, batch, t_dec, scale):
    n_heads, _, rank = q_lat.shape
    rope = q_pe.shape[2]
    n_pages = page_table.shape[1]
    page = pool_ckv.shape[1]
    n_pg = _pages_per_step(n_pages)
    n_steps = n_pages // n_pg
    rows = n_heads * t_dec
    page0 = lambda s: s * n_pg
    grid_spec = pltpu.PrefetchScalarGridSpec(
        num_scalar_prefetch=1,
        grid=(batch, n_steps),
        in_specs=[
            pl.BlockSpec((n_heads, t_dec, rank), lambda b, s, pt: (0, b, 0)),
            pl.BlockSpec((n_heads, t_dec, rope), lambda b, s, pt: (0, b, 0)),
            pl.BlockSpec((t_dec, rank), lambda b, s, pt: (b, 0)),
            pl.BlockSpec((None, rope, page), lambda b, s, pt: (b, 0, 0)),
        ] + _page_specs((None, page, rank), n_pg, page0) + _page_specs((None, rope, page), n_pg, page0),
        out_specs=pl.BlockSpec((n_heads, t_dec, rank), lambda b, s, pt: (0, b, 0)),
        scratch_shapes=[pltpu.VMEM((rows, 1), F32), pltpu.VMEM((rows, 1), F32), pltpu.VMEM((rows, rank), F32),
                        pltpu.VMEM((page, rank), F32)],
    )
    return pl.pallas_call(
        functools.partial(_mla_sample_kernel, n_pg=n_pg, n_heads=n_heads, t_dec=t_dec, n_steps=n_steps,
                          scale=scale),
        grid_spec=grid_spec,
        out_shape=jax.ShapeDtypeStruct((n_heads, batch * t_dec, rank), F32),
        compiler_params=_params("parallel", "arbitrary"),
        name="mla_sample_attn",
    )(page_table, q_lat, q_pe, ckv_new, kpe_new_t, *([pool_ckv] * n_pg), *([pool_kpe_t] * n_pg))


def _diff_sample_kernel(pt_ref, par_ref, q_ref, kn_ref, vn_ref, slope_ref, gq_ref, gs_ref, *refs,
                        n_pg, n_heads, t_dec, n_steps, past, out_scale):
    kp, vp = refs[:n_pg], refs[n_pg:2 * n_pg]
    o_ref, m_ref, l_ref, acc_ref, kbuf, vbuf = refs[2 * n_pg:]
    s = pl.program_id(1)
    rows = n_heads * t_dec
    page = kp[0].shape[0] // 2
    ck = 2 * page
    gq = gq_ref[...]
    q4 = q_ref[...].reshape(n_heads, 2, t_dec, HEAD_DIM)
    qs = [_rms_rows(q4[:, i].reshape(rows, HEAD_DIM), gq).astype(BF16) for i in range(2)]
    slope = slope_ref[...]
    t_row = lax.rem(lax.broadcasted_iota(jnp.int32, (rows, 1), 0), t_dec)
    col = lax.broadcasted_iota(jnp.int32, (1, ck), 1)

    @pl.when(s == 0)
    def _():
        m_ref[...] = jnp.full_like(m_ref, NEG_BIG)
        l_ref[...] = jnp.zeros_like(l_ref)
        acc_ref[...] = jnp.zeros_like(acc_ref)

    def update(k1, k2, v, dist, valid):
        bias = slope * dist
        for i, k in enumerate((k1, k2)):
            sc = _dot_t(qs[i], k) * (HEAD_DIM ** -0.5) - bias
            if valid is not None:
                sc = jnp.where(valid, sc, -jnp.inf)
            m, l, acc = _softmax_update(sc, v, m_ref[i], l_ref[i], acc_ref[i])
            m_ref[i] = m
            l_ref[i] = l
            acc_ref[i] = acc

    def halves(page_refs, i):
        sel = pl.ds(i, page, stride=2)
        return jnp.concatenate([r[sel, :] for r in page_refs], axis=0)

    wide = _pages_per_update(n_pg)
    col_w = lax.broadcasted_iota(jnp.int32, (1, wide * page), 1)
    for u in range(n_pg // wide):
        kpages = kp[u * wide:(u + 1) * wide]
        vpages = vp[u * wide:(u + 1) * wide]
        k1 = halves(kpages, 0).astype(BF16)
        k2 = halves(kpages, 1).astype(BF16)
        v = jnp.concatenate([halves(vpages, 0), halves(vpages, 1)], axis=-1).astype(BF16)
        kpos = (s * n_pg + u * wide) * page + col_w
        dist = (past + t_row - kpos).astype(F32)
        update(k1, k2, v, dist, None)

    @pl.when(s == n_steps - 1)
    def _():
        kbuf[...] = jnp.zeros_like(kbuf)
        vbuf[...] = jnp.zeros_like(vbuf)
        for i in range(2):
            kbuf[i, 0:t_dec] = kn_ref[i]
        vbuf[0:t_dec, 0:HEAD_DIM] = vn_ref[0]
        vbuf[0:t_dec, HEAD_DIM:] = vn_ref[1]
        dist = (t_row - col).astype(F32)
        update(kbuf[0].astype(BF16), kbuf[1].astype(BF16), vbuf[...].astype(BF16), dist, dist >= 0.0)
        lam = par_ref[0]
        o = acc_ref[0] / l_ref[0] - lam * (acc_ref[1] / l_ref[1])
        o = _rms_rows(o, gs_ref[...]) * out_scale
        o4 = jnp.stack([o[:, :HEAD_DIM].reshape(n_heads, t_dec, HEAD_DIM),
                        o[:, HEAD_DIM:].reshape(n_heads, t_dec, HEAD_DIM)], axis=1)
        o_ref[...] = o4.reshape(2 * n_heads, t_dec, HEAD_DIM).astype(o_ref.dtype)


def _diff_sample_attn(proj_hm, k_hm, pool_k, pool_v, page_table, par, slope_rows, gq, gs,
                      *, batch, t_dec, n_heads, out_scale):
    n_pages = page_table.shape[1]
    page = pool_k.shape[1] // 2
    n_pg = _pages_per_step(n_pages)
    n_steps = n_pages // n_pg
    rows = n_heads * t_dec
    ck = 2 * page
    page0 = lambda s: s * n_pg
    pg_block = (None, 2 * page, HEAD_DIM)
    grid_spec = pltpu.PrefetchScalarGridSpec(
        num_scalar_prefetch=1,
        grid=(batch, n_steps),
        in_specs=[
            pl.BlockSpec(memory_space=pltpu.SMEM),
            pl.BlockSpec((2 * n_heads, t_dec, HEAD_DIM), lambda b, s, pt: (0, b, 0)),
            pl.BlockSpec((2, t_dec, HEAD_DIM), lambda b, s, pt: (0, b, 0)),
            pl.BlockSpec((2, t_dec, HEAD_DIM), lambda b, s, pt: (n_heads + 1, b, 0)),
            pl.BlockSpec((rows, 1), lambda b, s, pt: (0, 0)),
            pl.BlockSpec((1, HEAD_DIM), lambda b, s, pt: (0, 0)),
            pl.BlockSpec((1, 2 * HEAD_DIM), lambda b, s, pt: (0, 0)),
        ] + _page_specs(pg_block, n_pg, page0) + _page_specs(pg_block, n_pg, page0),
        out_specs=pl.BlockSpec((2 * n_heads, t_dec, HEAD_DIM), lambda b, s, pt: (0, b, 0)),
        scratch_shapes=[pltpu.VMEM((2, rows, 1), F32), pltpu.VMEM((2, rows, 1), F32),
                        pltpu.VMEM((2, rows, 2 * HEAD_DIM), F32),
                        pltpu.VMEM((2, ck, HEAD_DIM), F32), pltpu.VMEM((ck, 2 * HEAD_DIM), F32)],
    )
    return pl.pallas_call(
        functools.partial(_diff_sample_kernel, n_pg=n_pg, n_heads=n_heads, t_dec=t_dec, n_steps=n_steps,
                          past=n_pages * page, out_scale=out_scale),
        grid_spec=grid_spec,
        out_shape=jax.ShapeDtypeStruct((2 * n_heads, batch * t_dec, HEAD_DIM), F32),
        compiler_params=_params("parallel", "arbitrary"),
        name="diff_sample_attn",
    )(page_table, par, proj_hm, k_hm, proj_hm, slope_rows, gq, gs, *([pool_k] * n_pg), *([pool_v] * n_pg))


def _to_token_major(x_hm, batch, t):
    s = x_hm.shape[0]
    return jnp.transpose(x_hm, (1, 0, 2)).reshape(batch, t, s, HEAD_DIM)


def _rms_glue(x, g):
    y = x * lax.rsqrt(jnp.mean(x * x, axis=-1, keepdims=True) + EPS)
    return y * g


def _rope_glue(x, pos):
    half = x.shape[-1] // 2
    inv = ROPE_THETA ** (-jnp.arange(half, dtype=F32) / half)
    ang = pos.astype(F32)[:, None] * inv
    ang = ang.reshape((pos.shape[0],) + (1,) * (x.ndim - 3) + (half,))
    cos, sin = jnp.cos(ang), jnp.sin(ang)
    x1, x2 = x[..., :half], x[..., half:]
    return jnp.concatenate([x1 * cos - x2 * sin, x1 * sin + x2 * cos], axis=-1)


def _tri(ck):
    j = lax.broadcasted_iota(jnp.int32, (ck, ck), 0)
    s = lax.broadcasted_iota(jnp.int32, (ck, ck), 1)
    return (j > s).astype(BF16)


def _sb_mixer(h, shape, w_in, w_out, pools, page_table):
    batch, t = shape
    d = h.shape[1]
    n_q = d // HEAD_DIM
    n_kv = (w_in.shape[1] - d) // (2 * HEAD_DIM)
    proj = _mm(h, w_in, out_dtype=F32, tm=1024, tn=512, out_head_major=True, name="sb_in")
    if pools is None:
        o_hm = _sb_prompt_attn(proj, _tri(min(KEY_CHUNK, t)), batch=batch, seq=t, n_q=n_q, n_kv=n_kv)
    else:
        pk, pv = (p.reshape(p.shape[0], p.shape[1] * n_kv, HEAD_DIM) for p in pools)
        o_hm = _sb_sample_attn(proj, pk, pv, page_table, _tri(2 * pools[0].shape[1]),
                               batch=batch, t_dec=t, n_q=n_q, n_kv=n_kv)
    o = _mm(o_hm, w_out, out_dtype=F32, tm=1024, tn=512, a_head_major=True, name="sb_out")
    k = _to_token_major(proj[n_q:n_q + n_kv], batch, t)
    v = _to_token_major(proj[n_q + n_kv:], batch, t)
    return o, k, v


def _mla_mixer(h, shape, w, pools, page_table):
    batch, t = shape
    m = batch * t
    w_in_a, w_in_pe, g_q_a, g_kv_a, g_k_pe, w_q_b, g_q_head, wk_abs, wv_abs, w_kv_flat, w_out = w
    n_heads, nope, rank = wk_abs.shape
    rope = w_in_pe.shape[1]
    q_rank = g_q_a.shape[0]
    dq = nope + rope
    past = 0 if page_table is None else page_table.shape[1] * pools[0].shape[1]
    q_pos = past + jnp.arange(t, dtype=jnp.int32)

    proj = _mm(h, w_in_a, out_dtype=F32, tm=1024, tn=512, name="mla_in")
    k_pe = _mm(h, w_in_pe, out_dtype=F32, tm=1024, tn=rope, name="mla_in_pe")
    q_a = _rms_glue(proj[:, :q_rank], g_q_a).astype(BF16)
    ckv = _rms_glue(proj[:, q_rank:], g_kv_a)
    kpe = _rope_glue(_rms_glue(k_pe, g_k_pe).reshape(batch, t, rope), q_pos).reshape(m, rope)
    q = _mm(q_a, w_q_b, out_dtype=F32, tm=1024, tn=512, name="mla_q_b").reshape(batch, t, n_heads, dq)
    q = _rms_glue(q, g_q_head)
    q_pe = _rope_glue(q[..., nope:], q_pos)
    q_nope_hm = jnp.transpose(q[..., :nope].reshape(m, n_heads, nope), (1, 0, 2)).astype(BF16)
    q_pe_hm = jnp.transpose(q_pe.reshape(m, n_heads, rope), (1, 0, 2))
    scale = dq ** -0.5
    if pools is None:
        kv_hm = _mm(ckv.astype(BF16), w_kv_flat, out_dtype=BF16, tm=1024, tn=512, out_head_major=True,
                    name="mla_kv_b")
        o_hm = _mla_prompt_attn(q_nope_hm, q_pe_hm.astype(BF16), kv_hm, kpe, batch=batch, seq=t, scale=scale)
    else:
        q_lat = _bmm(q_nope_hm, wk_abs, out_dtype=F32, tm=1024, name="mla_q_lat")
        page = pools[0].shape[1]
        kpe_new_t = jnp.pad(jnp.swapaxes(kpe.reshape(batch, t, rope), 1, 2), ((0, 0), (0, 0), (0, page - t)))
        ctx = _mla_sample_attn(q_lat, q_pe_hm, ckv, kpe_new_t, pools[0], jnp.swapaxes(pools[1], 1, 2),
                               page_table, batch=batch, t_dec=t, scale=scale)
        o_hm = _bmm(ctx, wv_abs, out_dtype=BF16, tm=1024, name="mla_ctx_v")
    o = _mm(o_hm, w_out, out_dtype=F32, tm=1024, tn=512, a_head_major=True, name="mla_out")
    return o, ckv.reshape(batch, t, rank), kpe.reshape(batch, t, rope)


def _diff_mixer(h, shape, layer, w, pools, page_table):
    batch, t = shape
    m = batch * t
    w_in, g_q, g_k, lam_q1, lam_k1, lam_q2, lam_k2, g_subln, w_out = w
    n_heads = w_out.shape[0] // (2 * HEAD_DIM)
    lam_init = 0.8 - 0.6 * math.exp(-0.3 * layer)
    ex = lambda a, c: jnp.exp(jnp.sum(a * c))
    lam = ex(lam_q1, lam_k1) - ex(lam_q2, lam_k2) + lam_init
    slopes = 2.0 ** (-8.0 * jnp.arange(1, n_heads + 1, dtype=F32) / n_heads)
    par = jnp.concatenate([lam[None], slopes]).astype(F32)
    gq = g_q.reshape(1, HEAD_DIM)
    gs = g_subln.reshape(1, 2 * HEAD_DIM)

    proj = _mm(h, w_in, out_dtype=F32, tm=1024, tn=512, out_head_major=True, name="diff_in")
    k_hm = _rms_glue(proj[2 * n_heads:2 * n_heads + 2], g_k)
    if pools is None:
        o_hm = _diff_prompt_attn(proj, k_hm, par, gq, gs, batch=batch, seq=t, n_heads=n_heads,
                                 out_scale=1.0 - lam_init)
    else:
        page = pools[0].shape[1]
        pk, pv = (p.reshape(p.shape[0], 2 * page, HEAD_DIM) for p in pools)
        slope_rows = jnp.repeat(slopes, t)[:, None]
        o_hm = _diff_sample_attn(proj, k_hm, pk, pv, page_table, par, slope_rows, gq, gs,
                                 batch=batch, t_dec=t, n_heads=n_heads, out_scale=1.0 - lam_init)
    o = _mm(o_hm, w_out, out_dtype=F32, tm=1024, tn=512, a_head_major=True, name="diff_out")
    k = _to_token_major(k_hm, batch, t).reshape(batch, t, 1, 2 * HEAD_DIM)
    v = _to_token_major(proj[2 * n_heads + 2:], batch, t).reshape(batch, t, 1, 2 * HEAD_DIM)
    return o, k, v


def kernel(x_prompt, x_sample, c_prompt, c_sample, cache_l0_k, cache_l0_v, cache_l1_ckv, cache_l1_kpe, cache_l2_k, cache_l2_v, cache_l3_k, cache_l3_v, page_table, w_ada, b_ada, ada_table, g_norm_mix, g_norm_ffn, w_in_sb, w_out_sb, w_in_mla, g_q_a, g_kv_a, g_k_pe, w_q_b, g_q_head, w_kv_b, w_out_mla, w_in_diff, g_q_diff, g_k_diff, lam_q1, lam_k1, lam_q2, lam_k2, g_subln, w_out_diff, w_gate_up, w_down):
    depth, n_chunks, d = ada_table.shape
    pools = ((cache_l0_k, cache_l0_v), (cache_l1_ckv, cache_l1_kpe),
             (cache_l2_k, cache_l2_v), (cache_l3_k, cache_l3_v))
    bf = lambda a: a.astype(BF16)

    n_p = c_prompt.shape[0]
    c_all = jnp.concatenate([c_prompt, c_sample], axis=0)
    silu_c = c_all * (1.0 / (1.0 + jnp.exp(-c_all)))
    mod_all = _mm(silu_c, bf(w_ada), out_dtype=F32, tm=c_all.shape[0], tn=512, name="ada") + b_ada
    mod_all = mod_all.reshape(c_all.shape[0], n_chunks, d)

    streams = [
        dict(x=x_prompt, mod=mod_all[:n_p], pools=None, bb=1, tt=256),
        dict(x=x_sample, mod=mod_all[n_p:], pools=pools, bb=32, tt=x_sample.shape[1]),
    ]

    q_rank, kv_rank, rope = g_q_a.shape[1], g_kv_a.shape[1], g_k_pe.shape[1]
    nope = g_q_head.shape[1] - rope

    new_rows = [[], []]
    pending = [None, None]
    for l in range(depth):
        kind, j = l % 3, l // 3
        w_gu, w_dn = bf(w_gate_up[l]), bf(w_down[l])
        if kind == 0:
            mix_w = (bf(w_in_sb[j]), bf(w_out_sb[j]))
        elif kind == 1:
            wkv = w_kv_b[j]
            mix_w = (bf(w_in_mla[j][:, :q_rank + kv_rank]), bf(w_in_mla[j][:, q_rank + kv_rank:]),
                     g_q_a[j], g_kv_a[j], g_k_pe[j], bf(w_q_b[j]), g_q_head[j],
                     bf(jnp.transpose(wkv[..., :nope], (1, 2, 0))),
                     bf(jnp.transpose(wkv[..., nope:], (1, 0, 2))),
                     bf(wkv.reshape(kv_rank, -1)),
                     bf(w_out_mla[j]))
        else:
            mix_w = (bf(w_in_diff[j]), g_q_diff[j], g_k_diff[j], lam_q1[j], lam_k1[j], lam_q2[j], lam_k2[j],
                     g_subln[j], bf(w_out_diff[j]))
        for si, st in enumerate(streams):
            x = st["x"]
            batch, t, _ = x.shape
            mod = st["mod"] + ada_table[l]
            chunk = lambda i: mod[:, i:i + 1, :]
            o_prev, gt_prev = pending[si] if pending[si] is not None else (None, None)
            x, h = _norm_stage(x, o_prev, gt_prev, g_norm_mix[l].reshape(1, 1, d), chunk(1), chunk(0),
                               bb=st["bb"], tt=st["tt"])
            tok_pools = None if st["pools"] is None else st["pools"][l]
            pt = None if st["pools"] is None else page_table
            if kind == 0:
                o, st_a, st_b = _sb_mixer(h, (batch, t), mix_w[0], mix_w[1], tok_pools, pt)
            elif kind == 1:
                o, st_a, st_b = _mla_mixer(h, (batch, t), mix_w, tok_pools, pt)
            else:
                o, st_a, st_b = _diff_mixer(h, (batch, t), l, mix_w, tok_pools, pt)
            new_rows[si] += [st_a, st_b]
            x, h = _norm_stage(x, o.reshape(batch, t, d), chunk(2), g_norm_ffn[l].reshape(1, 1, d),
                               chunk(4), chunk(3), bb=st["bb"], tt=st["tt"])
            o = _ffn(h, w_gu, w_dn, tm=512, tf=256)
            pending[si] = (o.reshape(batch, t, d), chunk(5))
            st["x"] = x

    outs = []
    for si, st in enumerate(streams):
        o, gt = pending[si]
        y, _ = _norm_stage(st["x"], o, gt, None, None, None, bb=st["bb"], tt=st["tt"])
        outs.append(y)
    return (outs[0], outs[1], *new_rows[0], *new_rows[1])
```

```python
import functools
import math

import jax
import jax.numpy as jnp
from jax import lax
from jax.experimental import pallas as pl
from jax.experimental.pallas import tpu as pltpu

F32 = jnp.float32
BF16 = jnp.bfloat16

HEAD_DIM = 128
EPS = 1e-6
ROPE_THETA = 10000.0
NEG_BIG = -1e30
SB_DEAD_LOG = -104.0
VMEM_LIMIT_BYTES = 56 * 1024 * 1024
Q_BLOCK = 256
SB_KEY_BLOCK = 512
SOFTMAX_BLOCK = 512
KEY_CHUNK = 256
HEADS_PER_ITER = 4
PAGES_PER_STEP = 32
SB_PAGES_PER_STEP = 64
PAGES_PER_UPDATE = 16
SB_PAGES_PER_UPDATE = 8


def _params(*sem):
    return pltpu.CompilerParams(dimension_semantics=sem, vmem_limit_bytes=VMEM_LIMIT_BYTES)


def _tile(n, want, align):
    if n <= want:
        return n
    t = (want // align) * align
    while t > 0 and n % t:
        t -= align
    assert t > 0, (n, want, align)
    return t


def _dot(a, b):
    return jnp.dot(a, b, preferred_element_type=F32)


def _dot_t(a, b):
    return lax.dot_general(a, b, (((1,), (1,)), ((), ())), preferred_element_type=F32)


def _mm_kernel(a_ref, w_ref, o_ref, *, a_slices, o_slices):
    if a_slices:
        a = jnp.concatenate([a_ref[h].astype(BF16) for h in range(a_slices)], axis=-1)
    else:
        a = a_ref[...].astype(BF16)
    acc = _dot(a, w_ref[...])
    if o_slices:
        for h in range(o_slices):
            o_ref[h] = acc[:, h * HEAD_DIM:(h + 1) * HEAD_DIM].astype(o_ref.dtype)
    else:
        o_ref[...] = acc.astype(o_ref.dtype)


def _mm(a, w, *, out_dtype, tm, tn, a_head_major=False, out_head_major=False, name="mm"):
    if a_head_major:
        s_in, m, _ = a.shape
        k = s_in * HEAD_DIM
    else:
        m, k = a.shape
        s_in = 0
    n = w.shape[1]
    assert w.shape[0] == k
    tm, tn = _tile(m, tm, 8), _tile(n, tn, HEAD_DIM)
    if a_head_major:
        a_spec = pl.BlockSpec((s_in, tm, HEAD_DIM), lambda i, j: (0, i, 0))
    else:
        a_spec = pl.BlockSpec((tm, k), lambda i, j: (i, 0))
    w_spec = pl.BlockSpec((k, tn), lambda i, j: (0, j))
    if out_head_major:
        assert tn % HEAD_DIM == 0
        s_out = tn // HEAD_DIM
        out_shape = jax.ShapeDtypeStruct((n // HEAD_DIM, m, HEAD_DIM), out_dtype)
        o_spec = pl.BlockSpec((s_out, tm, HEAD_DIM), lambda i, j: (j, i, 0))
    else:
        s_out = 0
        out_shape = jax.ShapeDtypeStruct((m, n), out_dtype)
        o_spec = pl.BlockSpec((tm, tn), lambda i, j: (i, j))
    return pl.pallas_call(
        functools.partial(_mm_kernel, a_slices=s_in, o_slices=s_out),
        grid=(m // tm, n // tn),
        in_specs=[a_spec, w_spec],
        out_specs=o_spec,
        out_shape=out_shape,
        compiler_params=_params("parallel", "arbitrary"),
        name=name,
    )(a, w)


def _bmm_kernel(a_ref, w_ref, o_ref):
    o_ref[...] = _dot(a_ref[...].astype(BF16), w_ref[...]).astype(o_ref.dtype)


def _bmm(a, w, *, out_dtype, tm, name="bmm"):
    h, m, k = a.shape
    n = w.shape[2]
    tm = _tile(m, tm, 8)
    return pl.pallas_call(
        _bmm_kernel,
        grid=(h, m // tm),
        in_specs=[pl.BlockSpec((None, tm, k), lambda g, i: (g, i, 0)),
                  pl.BlockSpec((None, k, n), lambda g, i: (g, 0, 0))],
        out_specs=pl.BlockSpec((None, tm, n), lambda g, i: (g, i, 0)),
        out_shape=jax.ShapeDtypeStruct((h, m, n), out_dtype),
        compiler_params=_params("parallel", "arbitrary"),
        name=name,
    )(a, w)


def _ffn_kernel(h_ref, wg_ref, wu_ref, wd_ref, o_ref):
    @pl.when(pl.program_id(1) == 0)
    def _():
        o_ref[...] = jnp.zeros_like(o_ref)

    h = h_ref[...]
    g = _dot(h, wg_ref[...])
    u = _dot(h, wu_ref[...])
    act = (g * (1.0 / (1.0 + jnp.exp(-g)))) * u
    o_ref[...] += _dot(act.astype(BF16), wd_ref[...])


def _ffn(h, w_gu, w_dn, *, tm, tf):
    m, d = h.shape
    d_ff = w_dn.shape[0]
    tm = _tile(m, tm, 8)
    assert d_ff % tf == 0
    nf = d_ff // tf
    return pl.pallas_call(
        _ffn_kernel,
        grid=(m // tm, nf),
        in_specs=[pl.BlockSpec((tm, d), lambda i, j: (i, 0)),
                  pl.BlockSpec((d, tf), lambda i, j: (0, j)),
                  pl.BlockSpec((d, tf), lambda i, j: (0, nf + j)),
                  pl.BlockSpec((tf, d), lambda i, j: (j, 0))],
        out_specs=pl.BlockSpec((tm, d), lambda i, j: (i, 0)),
        out_shape=jax.ShapeDtypeStruct((m, d), F32),
        compiler_params=_params("parallel", "arbitrary"),
        name="ffn",
    )(h, w_gu, w_gu, w_dn)


def _norm_kernel(*refs, has_resid, do_norm):
    refs = list(refs)
    x_ref = refs.pop(0)
    x = x_ref[...]
    if has_resid:
        o_ref = refs.pop(0)
        gt_ref = refs.pop(0)
        x = x + gt_ref[...] * o_ref[...]
    if do_norm:
        g_ref, sc_ref, sh_ref, xo_ref, h_ref = refs
    else:
        (xo_ref,) = refs
    xo_ref[...] = x
    if do_norm:
        ms = jnp.mean(x * x, axis=-1, keepdims=True)
        y = (x * lax.rsqrt(ms + EPS)) * g_ref[...]
        hm = y * (1.0 + sc_ref[...]) + sh_ref[...]
        bb, tt, d = hm.shape
        h_ref[...] = hm.reshape(bb * tt, d).astype(BF16)


def _norm_stage(x, o, gt, g, sc, sh, *, bb, tt):
    b, t, d = x.shape
    bb, tt = min(bb, b), min(tt, t)
    assert b % bb == 0 and t % tt == 0
    nt = t // tt
    has_resid = o is not None
    do_norm = g is not None
    big = pl.BlockSpec((bb, tt, d), lambda i, j: (i, j, 0))
    per_seq = pl.BlockSpec((bb, 1, d), lambda i, j: (i, 0, 0))
    args, specs = [x], [big]
    if has_resid:
        args += [o, gt]
        specs += [big, per_seq]
    out_shape = [jax.ShapeDtypeStruct((b, t, d), F32)]
    out_specs = [big]
    if do_norm:
        args += [g, sc, sh]
        specs += [pl.BlockSpec((1, 1, d), lambda i, j: (0, 0, 0)), per_seq, per_seq]
        out_shape.append(jax.ShapeDtypeStruct((b * t, d), BF16))
        out_specs.append(pl.BlockSpec((bb * tt, d), lambda i, j: (i * nt + j, 0)))
    res = pl.pallas_call(
        functools.partial(_norm_kernel, has_resid=has_resid, do_norm=do_norm),
        grid=(b // bb, nt),
        in_specs=specs,
        out_specs=out_specs,
        out_shape=out_shape,
        compiler_params=_params("parallel", "parallel"),
        name="norm_stage",
    )(*args)
    return (res[0], res[1]) if do_norm else (res[0], None)


def _sb_update(q, k, v, tri, strict, acc, carry, k_transposed=False):
    ck = tri.shape[0]
    n = v.shape[0] // ck
    z = (_dot(q, k) if k_transposed else _dot_t(q, k)) * (HEAD_DIM ** -0.5)
    lsig = jnp.minimum(z, 0.0) - jnp.log(1.0 + jnp.exp(-jnp.abs(z)))
    ls = lsig - z
    if strict is not None:
        ls = jnp.where(strict, ls, 0.0)
    hi = ls.astype(BF16)
    lo = (ls - hi.astype(F32)).astype(BF16)
    la = [None] * n
    for c in range(n - 1, -1, -1):
        sl = slice(c * ck, (c + 1) * ck)
        la[c] = _dot(hi[:, sl], tri) + _dot(lo[:, sl], tri) + carry
        carry = carry + jnp.sum(ls[:, sl], axis=-1, keepdims=True)
    la = la[0] if n == 1 else jnp.concatenate(la, axis=1)
    a = jnp.exp(lsig + la)
    if strict is not None:
        a = jnp.where(strict, a, 0.0)
    acc = acc + _dot(a.astype(BF16), v)
    return acc, carry


def _softmax_update(s, v, m, l, acc):
    m_new = jnp.maximum(m, jnp.max(s, axis=-1, keepdims=True))
    alpha = jnp.exp(m - m_new)
    p = jnp.exp(s - m_new)
    l = alpha * l + jnp.sum(p, axis=-1, keepdims=True)
    acc = alpha * acc + _dot(p.astype(BF16), v)
    return m_new, l, acc


def _merge_softmax_states(states):
    m = states[0][0]
    for st in states[1:]:
        m = jnp.maximum(m, st[0])
    l = acc = None
    for mi, li, ai in states:
        w = jnp.exp(mi - m)
        l = w * li if l is None else l + w * li
        acc = w * ai if acc is None else acc + w * ai
    return l, acc


def _unroll(trip):
    for n in (HEADS_PER_ITER, 2):
        if trip % n == 0:
            return n
    return 1


def _rms_rows(x, g):
    return (x * lax.rsqrt(jnp.mean(x * x, axis=-1, keepdims=True) + EPS)) * g


def _causal_pairs(n_blocks, tq=1, ck=1):
    qi, ci = [], []
    for q in range(n_blocks):
        for c in range((q * tq + tq - 1) // ck, -1, -1):
            qi.append(q)
            ci.append(c)
    return jnp.asarray(qi, jnp.int32), jnp.asarray(ci, jnp.int32)


def _block_positions(qi, c, tq, ck):
    qpos = qi * tq + lax.broadcasted_iota(jnp.int32, (tq, 1), 0)
    kpos = c * ck + lax.broadcasted_iota(jnp.int32, (1, ck), 1)
    return qpos, kpos


def _sb_prompt_kernel(qi_ref, ci_ref, q_ref, k_ref, v_ref, tri_ref, o_ref, acc_ref, carry_ref,
                      *, tq, ck, group):
    p = pl.program_id(2)
    qi, c = qi_ref[p], ci_ref[p]
    diag = (qi * tq + (tq - 1)) // ck

    @pl.when(c == diag)
    def _():
        acc_ref[...] = jnp.zeros_like(acc_ref)
        carry_ref[...] = jnp.zeros_like(carry_ref)

    k_t = k_ref[...].T.astype(BF16)
    v = v_ref[...].astype(BF16)
    tri = tri_ref[...]

    step = _unroll(group)

    def update(g, strict):
        acc, carry = _sb_update(q_ref[g].astype(BF16), k_t, v, tri, strict, acc_ref[g], carry_ref[g],
                                k_transposed=True)
        acc_ref[g] = acc
        carry_ref[g] = carry

    @pl.when(c == diag)
    def _():
        qpos, kpos = _block_positions(qi, c, tq, ck)
        strict = kpos < qpos

        def heads(gp, _):
            for r in range(step):
                update(gp * step + r, strict)
            return 0

        lax.fori_loop(0, group // step, heads, 0)

    @pl.when(c < diag)
    def _():
        def heads(gp, _):
            live = jnp.max(carry_ref[pl.ds(gp * step, step)]) > SB_DEAD_LOG

            @pl.when(live)
            def _():
                for r in range(step):
                    update(gp * step + r, None)

            return 0

        lax.fori_loop(0, group // step, heads, 0)

    @pl.when(c == 0)
    def _():
        o_ref[...] = acc_ref[...].astype(o_ref.dtype)


def _sb_prompt_attn(proj_hm, tri, *, batch, seq, n_q, n_kv):
    tq, ck = min(Q_BLOCK, seq), min(SB_KEY_BLOCK, seq)
    assert seq % tq == 0 and seq % ck == 0 and ck % tri.shape[0] == 0
    nb, nkb = seq // tq, seq // ck
    group = n_q // n_kv
    qi_tab, ci_tab = _causal_pairs(nb, tq, ck)
    grid_spec = pltpu.PrefetchScalarGridSpec(
        num_scalar_prefetch=2,
        grid=(batch, n_kv, int(qi_tab.shape[0])),
        in_specs=[
            pl.BlockSpec((group, tq, HEAD_DIM), lambda b, h, p, qi, ci: (h, b * nb + qi[p], 0)),
            pl.BlockSpec((None, ck, HEAD_DIM), lambda b, h, p, qi, ci: (n_q + h, b * nkb + ci[p], 0)),
            pl.BlockSpec((None, ck, HEAD_DIM), lambda b, h, p, qi, ci: (n_q + n_kv + h, b * nkb + ci[p], 0)),
            pl.BlockSpec(tri.shape, lambda b, h, p, qi, ci: (0, 0)),
        ],
        out_specs=pl.BlockSpec((group, tq, HEAD_DIM), lambda b, h, p, qi, ci: (h, b * nb + qi[p], 0)),
        scratch_shapes=[pltpu.VMEM((group, tq, HEAD_DIM), F32), pltpu.VMEM((group, tq, 1), F32)],
    )
    return pl.pallas_call(
        functools.partial(_sb_prompt_kernel, tq=tq, ck=ck, group=group),
        grid_spec=grid_spec,
        out_shape=jax.ShapeDtypeStruct((n_q, batch * seq, HEAD_DIM), BF16),
        compiler_params=_params("parallel", "parallel", "arbitrary"),
        name="sb_prompt_attn",
    )(qi_tab, ci_tab, proj_hm, proj_hm, proj_hm, tri)


def _diff_prompt_kernel(qi_ref, ci_ref, par_ref, q_ref, k_ref, v_ref, gq_ref, gs_ref, o_ref,
                        m_ref, l_ref, acc_ref, *, tq, ck, group, head_base, out_scale):
    p = pl.program_id(2)
    hg = pl.program_id(1)
    qi, c = qi_ref[p], ci_ref[p]

    @pl.when(c == qi)
    def _():
        m_ref[...] = jnp.full_like(m_ref, NEG_BIG)
        l_ref[...] = jnp.zeros_like(l_ref)
        acc_ref[...] = jnp.zeros_like(acc_ref)

    qpos, kpos = _block_positions(qi, c, tq, ck)
    dist = (qpos - kpos).astype(F32)
    v = jnp.concatenate([v_ref[0], v_ref[1]], axis=-1).astype(BF16)
    ks_t = (k_ref[0].T.astype(BF16), k_ref[1].T.astype(BF16))
    gq = gq_ref[...]

    def sweep(masked):
        def head(g, _):
            slope = par_ref[head_base + hg * group + g]
            bias = slope * dist
            for i in range(2):
                qn = _rms_rows(q_ref[2 * g + i], gq).astype(BF16)
                s = _dot(qn, ks_t[i]) * (HEAD_DIM ** -0.5) - bias
                if masked:
                    s = jnp.where(dist >= 0.0, s, -jnp.inf)
                m, l, acc = _softmax_update(s, v, m_ref[i, g], l_ref[i, g], acc_ref[i, g])
                m_ref[i, g] = m
                l_ref[i, g] = l
                acc_ref[i, g] = acc
            return 0

        lax.fori_loop(0, group, head, 0)

    @pl.when(c == qi)
    def _():
        sweep(True)

    @pl.when(c < qi)
    def _():
        sweep(False)

    @pl.when(c == 0)
    def _():
        lam = par_ref[0]
        gs = gs_ref[...]

        def fin(g, _):
            o = acc_ref[0, g] / l_ref[0, g] - lam * (acc_ref[1, g] / l_ref[1, g])
            o = _rms_rows(o, gs) * out_scale
            o_ref[2 * g] = o[:, :HEAD_DIM].astype(o_ref.dtype)
            o_ref[2 * g + 1] = o[:, HEAD_DIM:].astype(o_ref.dtype)
            return 0

        lax.fori_loop(0, group, fin, 0)


def _diff_prompt_attn(proj_hm, k_hm, par, gq, gs, *, batch, seq, n_heads, out_scale):
    tq = ck = min(SOFTMAX_BLOCK, seq)
    assert seq % tq == 0
    nb = seq // tq
    n_groups = 2 if n_heads % 2 == 0 else 1
    group = n_heads // n_groups
    qi_tab, ci_tab = _causal_pairs(nb)
    v_blk = (2 * n_heads + 2) // 2
    grid_spec = pltpu.PrefetchScalarGridSpec(
        num_scalar_prefetch=2,
        grid=(batch, n_groups, int(qi_tab.shape[0])),
        in_specs=[
            pl.BlockSpec(memory_space=pltpu.SMEM),
            pl.BlockSpec((2 * group, tq, HEAD_DIM), lambda b, h, p, qi, ci: (h, b * nb + qi[p], 0)),
            pl.BlockSpec((2, ck, HEAD_DIM), lambda b, h, p, qi, ci: (0, b * nb + ci[p], 0)),
            pl.BlockSpec((2, ck, HEAD_DIM), lambda b, h, p, qi, ci: (v_blk, b * nb + ci[p], 0)),
            pl.BlockSpec((1, HEAD_DIM), lambda b, h, p, qi, ci: (0, 0)),
            pl.BlockSpec((1, 2 * HEAD_DIM), lambda b, h, p, qi, ci: (0, 0)),
        ],
        out_specs=pl.BlockSpec((2 * group, tq, HEAD_DIM), lambda b, h, p, qi, ci: (h, b * nb + qi[p], 0)),
        scratch_shapes=[pltpu.VMEM((2, group, tq, 1), F32), pltpu.VMEM((2, group, tq, 1), F32),
                        pltpu.VMEM((2, group, tq, 2 * HEAD_DIM), F32)],
    )
    return pl.pallas_call(
        functools.partial(_diff_prompt_kernel, tq=tq, ck=ck, group=group, head_base=1,
                          out_scale=out_scale),
        grid_spec=grid_spec,
        out_shape=jax.ShapeDtypeStruct((2 * n_heads, batch * seq, HEAD_DIM), BF16),
        compiler_params=_params("parallel", "parallel", "arbitrary"),
        name="diff_prompt_attn",
    )(qi_tab, ci_tab, par, proj_hm, k_hm, proj_hm, gq, gs)


def _mla_prompt_kernel(qi_ref, ci_ref, qn_ref, qp_ref, kv_ref, kpe_ref, o_ref, m_ref, l_ref, acc_ref,
                       *, tq, ck, group, scale):
    p = pl.program_id(2)
    qi, c = qi_ref[p], ci_ref[p]

    @pl.when(c == qi)
    def _():
        m_ref[...] = jnp.full_like(m_ref, NEG_BIG)
        l_ref[...] = jnp.zeros_like(l_ref)
        acc_ref[...] = jnp.zeros_like(acc_ref)

    kpe = kpe_ref[...].astype(BF16)

    def sweep(valid):
        def head(g, _):
            s = (_dot_t(qn_ref[g], kv_ref[2 * g]) + _dot_t(qp_ref[g], kpe)) * scale
            if valid is not None:
                s = jnp.where(valid, s, -jnp.inf)
            m, l, acc = _softmax_update(s, kv_ref[2 * g + 1], m_ref[g], l_ref[g], acc_ref[g])
            m_ref[g] = m
            l_ref[g] = l
            acc_ref[g] = acc
            return 0

        lax.fori_loop(0, group, head, 0, unroll=_unroll(group))

    @pl.when(c == qi)
    def _():
        qpos, kpos = _block_positions(qi, c, tq, ck)
        sweep(kpos <= qpos)

    @pl.when(c < qi)
    def _():
        sweep(None)

    @pl.when(c == 0)
    def _():
        o_ref[...] = (acc_ref[...] / l_ref[...]).astype(o_ref.dtype)


def _mla_prompt_attn(q_nope, q_pe, kv_hm, kpe, *, batch, seq, scale):
    n_heads, _, nope = q_nope.shape
    rope = q_pe.shape[2]
    tq = ck = min(SOFTMAX_BLOCK, seq)
    assert seq % tq == 0
    nb = seq // tq
    n_groups = 2 if n_heads % 2 == 0 else 1
    group = n_heads // n_groups
    qi_tab, ci_tab = _causal_pairs(nb)
    grid_spec = pltpu.PrefetchScalarGridSpec(
        num_scalar_prefetch=2,
        grid=(batch, n_groups, int(qi_tab.shape[0])),
        in_specs=[
            pl.BlockSpec((group, tq, nope), lambda b, h, p, qi, ci: (h, b * nb + qi[p], 0)),
            pl.BlockSpec((group, tq, rope), lambda b, h, p, qi, ci: (h, b * nb + qi[p], 0)),
            pl.BlockSpec((2 * group, ck, HEAD_DIM), lambda b, h, p, qi, ci: (h, b * nb + ci[p], 0)),
            pl.BlockSpec((ck, rope), lambda b, h, p, qi, ci: (b * nb + ci[p], 0)),
        ],
        out_specs=pl.BlockSpec((group, tq, HEAD_DIM), lambda b, h, p, qi, ci: (h, b * nb + qi[p], 0)),
        scratch_shapes=[pltpu.VMEM((group, tq, 1), F32), pltpu.VMEM((group, tq, 1), F32),
                        pltpu.VMEM((group, tq, HEAD_DIM), F32)],
    )
    return pl.pallas_call(
        functools.partial(_mla_prompt_kernel, tq=tq, ck=ck, group=group, scale=scale),
        grid_spec=grid_spec,
        out_shape=jax.ShapeDtypeStruct((n_heads, batch * seq, HEAD_DIM), BF16),
        compiler_params=_params("parallel", "parallel", "arbitrary"),
        name="mla_prompt_attn",
    )(qi_tab, ci_tab, q_nope, q_pe, kv_hm, kpe)


def _pages_per_step(n_pages, limit=PAGES_PER_STEP):
    p = min(limit, n_pages)
    while n_pages % p or p % 2:
        p -= 1
    assert p >= 2, "page count must be even"
    return p


def _pages_per_update(n_pg, limit=PAGES_PER_UPDATE):
    w = min(limit, n_pg)
    while n_pg % w or w % 2:
        w -= 1
    return w


def _page_specs(block, n, step_page0):
    trail = (0,) * (len(block) - 1)
    return [pl.BlockSpec(block, (lambda b, s, pt, r=r: (pt[b, step_page0(s) + r],) + trail))
            for r in range(n)]


def _sb_sample_kernel(pt_ref, q_ref, kn_ref, vn_ref, tri_ref, *refs, n_pg, n_kv, group, t_dec, n_steps):
    kp, vp = refs[:n_pg], refs[n_pg:2 * n_pg]
    o_ref, acc_ref, carry_ref, kbuf, vbuf = refs[2 * n_pg:]
    s = pl.program_id(1)
    rows = group * t_dec
    page = kp[0].shape[0] // n_kv
    tri = tri_ref[...]

    def q_rows(h):
        return q_ref[h * group:(h + 1) * group].reshape(rows, HEAD_DIM).astype(BF16)

    @pl.when(s == 0)
    def _():
        t_row = lax.rem(lax.broadcasted_iota(jnp.int32, (rows, 1), 0), t_dec)
        col = lax.broadcasted_iota(jnp.int32, (1, kbuf.shape[0]), 1)
        strict = col < t_row
        for h in range(n_kv):
            kbuf[...] = jnp.zeros_like(kbuf)
            vbuf[...] = jnp.zeros_like(vbuf)
            kbuf[0:t_dec] = kn_ref[h]
            vbuf[0:t_dec] = vn_ref[h]
            acc, carry = _sb_update(q_rows(h), kbuf[...].astype(BF16), vbuf[...].astype(BF16), tri, strict,
                                    jnp.zeros((rows, HEAD_DIM), F32), jnp.zeros((rows, 1), F32))
            acc_ref[h] = acc
            carry_ref[h] = carry

    wide = _pages_per_update(n_pg, SB_PAGES_PER_UPDATE)

    for u in range(n_pg // wide - 1, -1, -1):
        @pl.when(jnp.max(carry_ref[...]) > SB_DEAD_LOG)
        def _(u=u):
            pages = range(u * wide, (u + 1) * wide)
            for h in range(n_kv):
                sel = pl.ds(h, page, stride=n_kv)
                k = jnp.concatenate([kp[r][sel, :] for r in pages], axis=0).astype(BF16)
                v = jnp.concatenate([vp[r][sel, :] for r in pages], axis=0).astype(BF16)
                acc, carry = _sb_update(q_rows(h), k, v, tri, None, acc_ref[h], carry_ref[h])
                acc_ref[h] = acc
                carry_ref[h] = carry

    @pl.when(s == n_steps - 1)
    def _():
        for h in range(n_kv):
            o_ref[h * group:(h + 1) * group] = acc_ref[h].reshape(group, t_dec, HEAD_DIM).astype(o_ref.dtype)


def _sb_sample_attn(proj_hm, pool_k, pool_v, page_table, tri, *, batch, t_dec, n_q, n_kv):
    n_pages = page_table.shape[1]
    n_pg = _pages_per_step(n_pages, SB_PAGES_PER_STEP)
    n_steps = n_pages // n_pg
    group = n_q // n_kv
    rows = group * t_dec
    ck = tri.shape[0]
    pg_block = (None,) + pool_k.shape[1:]
    page0 = lambda s: (n_steps - 1 - s) * n_pg
    grid_spec = pltpu.PrefetchScalarGridSpec(
        num_scalar_prefetch=1,
        grid=(batch, n_steps),
        in_specs=[
            pl.BlockSpec((n_q, t_dec, HEAD_DIM), lambda b, s, pt: (0, b, 0)),
            pl.BlockSpec((n_kv, t_dec, HEAD_DIM), lambda b, s, pt: (n_q // n_kv, b, 0)),
            pl.BlockSpec((n_kv, t_dec, HEAD_DIM), lambda b, s, pt: (n_q // n_kv + 1, b, 0)),
            pl.BlockSpec((ck, ck), lambda b, s, pt: (0, 0)),
        ] + _page_specs(pg_block, n_pg, page0) + _page_specs(pg_block, n_pg, page0),
        out_specs=pl.BlockSpec((n_q, t_dec, HEAD_DIM), lambda b, s, pt: (0, b, 0)),
        scratch_shapes=[pltpu.VMEM((n_kv, rows, HEAD_DIM), F32), pltpu.VMEM((n_kv, rows, 1), F32),
                        pltpu.VMEM((ck, HEAD_DIM), F32), pltpu.VMEM((ck, HEAD_DIM), F32)],
    )
    return pl.pallas_call(
        functools.partial(_sb_sample_kernel, n_pg=n_pg, n_kv=n_kv, group=group, t_dec=t_dec, n_steps=n_steps),
        grid_spec=grid_spec,
        out_shape=jax.ShapeDtypeStruct((n_q, batch * t_dec, HEAD_DIM), F32),
        compiler_params=_params("parallel", "arbitrary"),
        name="sb_sample_attn",
    )(page_table, proj_hm, proj_hm, proj_hm, tri, *([pool_k] * n_pg), *([pool_v] * n_pg))


def _mla_sample_kernel(pt_ref, ql_ref, qp_ref, cn_ref, knt_ref, *refs, n_pg, n_heads, t_dec, n_steps, scale):
    cp, kp = refs[:n_pg], refs[n_pg:2 * n_pg]
    o_ref, m_ref, l_ref, acc_ref, cbuf = refs[2 * n_pg:]
    s = pl.program_id(1)
    rows = n_heads * t_dec
    ql = ql_ref[...].reshape(rows, ql_ref.shape[2]).astype(BF16)
    qp = qp_ref[...].reshape(rows, qp_ref.shape[2]).astype(BF16)

    @pl.when(s == 0)
    def _():
        m_ref[...] = jnp.full_like(m_ref, NEG_BIG)
        l_ref[...] = jnp.zeros_like(l_ref)
        acc_ref[...] = jnp.zeros_like(acc_ref)

    def update(ckv, kpe_t, valid, state):
        sc = (_dot_t(ql, ckv) + _dot(qp, kpe_t)) * scale
        if valid is not None:
            sc = jnp.where(valid, sc, -jnp.inf)
        return _softmax_update(sc, ckv, *state)

    n_st = m_ref.shape[0]
    states = [(m_ref[i], l_ref[i], acc_ref[i]) for i in range(n_st)]
    wide = _pages_per_update(n_pg)
    for u in range(n_pg // wide):
        pages = range(u * wide, (u + 1) * wide)
        ckv = jnp.concatenate([cp[r][...] for r in pages], axis=0).astype(BF16)
        kpe_t = jnp.concatenate([kp[r][...] for r in pages], axis=1).astype(BF16)
        states[u % n_st] = update(ckv, kpe_t, None, states[u % n_st])
    for i in range(n_st):
        m_ref[i], l_ref[i], acc_ref[i] = states[i]

    @pl.when(s == n_steps - 1)
    def _():
        t_row = lax.rem(lax.broadcasted_iota(jnp.int32, (rows, 1), 0), t_dec)
        col = lax.broadcasted_iota(jnp.int32, (1, cbuf.shape[0]), 1)
        valid = col <= t_row
        cbuf[...] = jnp.zeros_like(cbuf)
        cbuf[0:t_dec] = cn_ref[...]
        first = update(cbuf[...].astype(BF16), knt_ref[...].astype(BF16), valid,
                       (m_ref[0], l_ref[0], acc_ref[0]))
        l, acc = _merge_softmax_states([first] + [(m_ref[i], l_ref[i], acc_ref[i]) for i in range(1, n_st)])
        o = acc / l
        o_ref[...] = o.reshape(n_heads, t_dec, o.shape[1]).astype(o_ref.dtype)


def _mla_sample_attn(q_lat, q_pe, ckv_new, kpe_new_t, pool_ckv, pool_kpe_t, page_table, *, batch, t_dec, scale):
    n_heads, _, rank = q_lat.shape
    rope = q_pe.shape[2]
    n_pages = page_table.shape[1]
    page = pool_ckv.shape[1]
    n_pg = _pages_per_step(n_pages)
    n_steps = n_pages // n_pg
    rows = n_heads * t_dec
    n_st = min(2, n_pg // _pages_per_update(n_pg))
    page0 = lambda s: s * n_pg
    grid_spec = pltpu.PrefetchScalarGridSpec(
        num_scalar_prefetch=1,
        grid=(batch, n_steps),
        in_specs=[
            pl.BlockSpec((n_heads, t_dec, rank), lambda b, s, pt: (0, b, 0)),
            pl.BlockSpec((n_heads, t_dec, rope), lambda b, s, pt: (0, b, 0)),
            pl.BlockSpec((t_dec, rank), lambda b, s, pt: (b, 0)),
            pl.BlockSpec((None, rope, page), lambda b, s, pt: (b, 0, 0)),
        ] + _page_specs((None, page, rank), n_pg, page0) + _page_specs((None, rope, page), n_pg, page0),
        out_specs=pl.BlockSpec((n_heads, t_dec, rank), lambda b, s, pt: (0, b, 0)),
        scratch_shapes=[pltpu.VMEM((n_st, rows, 1), F32), pltpu.VMEM((n_st, rows, 1), F32),
                        pltpu.VMEM((n_st, rows, rank), F32), pltpu.VMEM((page, rank), F32)],
    )
    return pl.pallas_call(
        functools.partial(_mla_sample_kernel, n_pg=n_pg, n_heads=n_heads, t_dec=t_dec, n_steps=n_steps,
                          scale=scale),
        grid_spec=grid_spec,
        out_shape=jax.ShapeDtypeStruct((n_heads, batch * t_dec, rank), F32),
        compiler_params=_params("parallel", "arbitrary"),
        name="mla_sample_attn",
    )(page_table, q_lat, q_pe, ckv_new, kpe_new_t, *([pool_ckv] * n_pg), *([pool_kpe_t] * n_pg))


def _diff_sample_kernel(pt_ref, par_ref, q_ref, kn_ref, vn_ref, slope_ref, gq_ref, gs_ref, *refs,
                        n_pg, n_heads, t_dec, n_steps, past, out_scale):
    kp, vp = refs[:n_pg], refs[n_pg:2 * n_pg]
    o_ref, m_ref, l_ref, acc_ref, kbuf, vbuf = refs[2 * n_pg:]
    s = pl.program_id(1)
    rows = n_heads * t_dec
    page = kp[0].shape[0] // 2
    ck = 2 * page
    gq = gq_ref[...]
    q4 = q_ref[...].reshape(n_heads, 2, t_dec, HEAD_DIM)
    qs = [_rms_rows(q4[:, i].reshape(rows, HEAD_DIM), gq).astype(BF16) for i in range(2)]
    slope = slope_ref[...]
    t_row = lax.rem(lax.broadcasted_iota(jnp.int32, (rows, 1), 0), t_dec)
    col = lax.broadcasted_iota(jnp.int32, (1, ck), 1)

    @pl.when(s == 0)
    def _():
        m_ref[...] = jnp.full_like(m_ref, NEG_BIG)
        l_ref[...] = jnp.zeros_like(l_ref)
        acc_ref[...] = jnp.zeros_like(acc_ref)

    def update(k1, k2, v, dist, valid, state):
        bias = slope * dist
        out = []
        for i, k in enumerate((k1, k2)):
            sc = _dot_t(qs[i], k) * (HEAD_DIM ** -0.5) - bias
            if valid is not None:
                sc = jnp.where(valid, sc, -jnp.inf)
            out.append(_softmax_update(sc, v, *state[i]))
        return out

    def load_state(j):
        return [(m_ref[j, i], l_ref[j, i], acc_ref[j, i]) for i in range(2)]

    def halves(page_refs, i):
        sel = pl.ds(i, page, stride=2)
        return jnp.concatenate([r[sel, :] for r in page_refs], axis=0)

    n_st = m_ref.shape[0]
    states = [load_state(j) for j in range(n_st)]
    wide = _pages_per_update(n_pg)
    col_w = lax.broadcasted_iota(jnp.int32, (1, wide * page), 1)
    for u in range(n_pg // wide):
        kpages = kp[u * wide:(u + 1) * wide]
        vpages = vp[u * wide:(u + 1) * wide]
        k1 = halves(kpages, 0).astype(BF16)
        k2 = halves(kpages, 1).astype(BF16)
        v = jnp.concatenate([halves(vpages, 0), halves(vpages, 1)], axis=-1).astype(BF16)
        kpos = (s * n_pg + u * wide) * page + col_w
        dist = (past + t_row - kpos).astype(F32)
        states[u % n_st] = update(k1, k2, v, dist, None, states[u % n_st])
    for j in range(n_st):
        for i in range(2):
            m_ref[j, i], l_ref[j, i], acc_ref[j, i] = states[j][i]

    @pl.when(s == n_steps - 1)
    def _():
        kbuf[...] = jnp.zeros_like(kbuf)
        vbuf[...] = jnp.zeros_like(vbuf)
        for i in range(2):
            kbuf[i, 0:t_dec] = kn_ref[i]
        vbuf[0:t_dec, 0:HEAD_DIM] = vn_ref[0]
        vbuf[0:t_dec, HEAD_DIM:] = vn_ref[1]
        dist = (t_row - col).astype(F32)
        first = update(kbuf[0].astype(BF16), kbuf[1].astype(BF16), vbuf[...].astype(BF16), dist, dist >= 0.0,
                       load_state(0))
        rest = [load_state(j) for j in range(1, n_st)]
        (l1, acc1), (l2, acc2) = (_merge_softmax_states([first[i]] + [r[i] for r in rest]) for i in range(2))
        lam = par_ref[0]
        o = acc1 / l1 - lam * (acc2 / l2)
        o = _rms_rows(o, gs_ref[...]) * out_scale
        o4 = jnp.stack([o[:, :HEAD_DIM].reshape(n_heads, t_dec, HEAD_DIM),
                        o[:, HEAD_DIM:].reshape(n_heads, t_dec, HEAD_DIM)], axis=1)
        o_ref[...] = o4.reshape(2 * n_heads, t_dec, HEAD_DIM).astype(o_ref.dtype)


def _diff_sample_attn(proj_hm, k_hm, pool_k, pool_v, page_table, par, slope_rows, gq, gs,
                      *, batch, t_dec, n_heads, out_scale):
    n_pages = page_table.shape[1]
    page = pool_k.shape[1] // 2
    n_pg = _pages_per_step(n_pages)
    n_steps = n_pages // n_pg
    rows = n_heads * t_dec
    ck = 2 * page
    n_st = min(2, n_pg // _pages_per_update(n_pg))
    page0 = lambda s: s * n_pg
    pg_block = (None, 2 * page, HEAD_DIM)
    grid_spec = pltpu.PrefetchScalarGridSpec(
        num_scalar_prefetch=1,
        grid=(batch, n_steps),
        in_specs=[
            pl.BlockSpec(memory_space=pltpu.SMEM),
            pl.BlockSpec((2 * n_heads, t_dec, HEAD_DIM), lambda b, s, pt: (0, b, 0)),
            pl.BlockSpec((2, t_dec, HEAD_DIM), lambda b, s, pt: (0, b, 0)),
            pl.BlockSpec((2, t_dec, HEAD_DIM), lambda b, s, pt: (n_heads + 1, b, 0)),
            pl.BlockSpec((rows, 1), lambda b, s, pt: (0, 0)),
            pl.BlockSpec((1, HEAD_DIM), lambda b, s, pt: (0, 0)),
            pl.BlockSpec((1, 2 * HEAD_DIM), lambda b, s, pt: (0, 0)),
        ] + _page_specs(pg_block, n_pg, page0) + _page_specs(pg_block, n_pg, page0),
        out_specs=pl.BlockSpec((2 * n_heads, t_dec, HEAD_DIM), lambda b, s, pt: (0, b, 0)),
        scratch_shapes=[pltpu.VMEM((n_st, 2, rows, 1), F32), pltpu.VMEM((n_st, 2, rows, 1), F32),
                        pltpu.VMEM((n_st, 2, rows, 2 * HEAD_DIM), F32),
                        pltpu.VMEM((2, ck, HEAD_DIM), F32), pltpu.VMEM((ck, 2 * HEAD_DIM), F32)],
    )
    return pl.pallas_call(
        functools.partial(_diff_sample_kernel, n_pg=n_pg, n_heads=n_heads, t_dec=t_dec, n_steps=n_steps,
                          past=n_pages * page, out_scale=out_scale),
        grid_spec=grid_spec,
        out_shape=jax.ShapeDtypeStruct((2 * n_heads, batch * t_dec, HEAD_DIM), F32),
        compiler_params=_params("parallel", "arbitrary"),
        name="diff_sample_attn",
    )(page_table, par, proj_hm, k_hm, proj_hm, slope_rows, gq, gs, *([pool_k] * n_pg), *([pool_v] * n_pg))


def _to_token_major(x_hm, batch, t):
    s = x_hm.shape[0]
    return jnp.transpose(x_hm, (1, 0, 2)).reshape(batch, t, s, HEAD_DIM)


def _rms_glue(x, g):
    y = x * lax.rsqrt(jnp.mean(x * x, axis=-1, keepdims=True) + EPS)
    return y * g


def _rope_glue(x, pos):
    half = x.shape[-1] // 2
    inv = ROPE_THETA ** (-jnp.arange(half, dtype=F32) / half)
    ang = pos.astype(F32)[:, None] * inv
    ang = ang.reshape((pos.shape[0],) + (1,) * (x.ndim - 3) + (half,))
    cos, sin = jnp.cos(ang), jnp.sin(ang)
    x1, x2 = x[..., :half], x[..., half:]
    return jnp.concatenate([x1 * cos - x2 * sin, x1 * sin + x2 * cos], axis=-1)


def _tri(ck):
    j = lax.broadcasted_iota(jnp.int32, (ck, ck), 0)
    s = lax.broadcasted_iota(jnp.int32, (ck, ck), 1)
    return (j > s).astype(BF16)


def _sb_mixer(h, shape, w_in, w_out, pools, page_table):
    batch, t = shape
    d = h.shape[1]
    n_q = d // HEAD_DIM
    n_kv = (w_in.shape[1] - d) // (2 * HEAD_DIM)
    proj = _mm(h, w_in, out_dtype=F32, tm=1024, tn=512, out_head_major=True, name="sb_in")
    if pools is None:
        o_hm = _sb_prompt_attn(proj, _tri(min(KEY_CHUNK, t)), batch=batch, seq=t, n_q=n_q, n_kv=n_kv)
    else:
        pk, pv = (p.reshape(p.shape[0], p.shape[1] * n_kv, HEAD_DIM) for p in pools)
        o_hm = _sb_sample_attn(proj, pk, pv, page_table, _tri(2 * pools[0].shape[1]),
                               batch=batch, t_dec=t, n_q=n_q, n_kv=n_kv)
    o = _mm(o_hm, w_out, out_dtype=F32, tm=1024, tn=512, a_head_major=True, name="sb_out")
    k = _to_token_major(proj[n_q:n_q + n_kv], batch, t)
    v = _to_token_major(proj[n_q + n_kv:], batch, t)
    return o, k, v


def _mla_mixer(h, shape, w, pools, page_table):
    batch, t = shape
    m = batch * t
    w_in_a, w_in_pe, g_q_a, g_kv_a, g_k_pe, w_q_b, g_q_head, wk_abs, wv_abs, w_kv_flat, w_out = w
    n_heads, nope, rank = wk_abs.shape
    rope = w_in_pe.shape[1]
    q_rank = g_q_a.shape[0]
    dq = nope + rope
    past = 0 if page_table is None else page_table.shape[1] * pools[0].shape[1]
    q_pos = past + jnp.arange(t, dtype=jnp.int32)

    proj = _mm(h, w_in_a, out_dtype=F32, tm=1024, tn=512, name="mla_in")
    k_pe = _mm(h, w_in_pe, out_dtype=F32, tm=1024, tn=rope, name="mla_in_pe")
    q_a = _rms_glue(proj[:, :q_rank], g_q_a).astype(BF16)
    ckv = _rms_glue(proj[:, q_rank:], g_kv_a)
    kpe = _rope_glue(_rms_glue(k_pe, g_k_pe).reshape(batch, t, rope), q_pos).reshape(m, rope)
    q = _mm(q_a, w_q_b, out_dtype=F32, tm=1024, tn=512, name="mla_q_b").reshape(batch, t, n_heads, dq)
    q = _rms_glue(q, g_q_head)
    q_pe = _rope_glue(q[..., nope:], q_pos)
    q_nope_hm = jnp.transpose(q[..., :nope].reshape(m, n_heads, nope), (1, 0, 2)).astype(BF16)
    q_pe_hm = jnp.transpose(q_pe.reshape(m, n_heads, rope), (1, 0, 2))
    scale = dq ** -0.5
    if pools is None:
        kv_hm = _mm(ckv.astype(BF16), w_kv_flat, out_dtype=BF16, tm=1024, tn=512, out_head_major=True,
                    name="mla_kv_b")
        o_hm = _mla_prompt_attn(q_nope_hm, q_pe_hm.astype(BF16), kv_hm, kpe, batch=batch, seq=t, scale=scale)
    else:
        q_lat = _bmm(q_nope_hm, wk_abs, out_dtype=F32, tm=1024, name="mla_q_lat")
        page = pools[0].shape[1]
        kpe_new_t = jnp.pad(jnp.swapaxes(kpe.reshape(batch, t, rope), 1, 2), ((0, 0), (0, 0), (0, page - t)))
        ctx = _mla_sample_attn(q_lat, q_pe_hm, ckv, kpe_new_t, pools[0], jnp.swapaxes(pools[1], 1, 2),
                               page_table, batch=batch, t_dec=t, scale=scale)
        o_hm = _bmm(ctx, wv_abs, out_dtype=BF16, tm=1024, name="mla_ctx_v")
    o = _mm(o_hm, w_out, out_dtype=F32, tm=1024, tn=512, a_head_major=True, name="mla_out")
    return o, ckv.reshape(batch, t, rank), kpe.reshape(batch, t, rope)


def _diff_mixer(h, shape, layer, w, pools, page_table):
    batch, t = shape
    m = batch * t
    w_in, g_q, g_k, lam_q1, lam_k1, lam_q2, lam_k2, g_subln, w_out = w
    n_heads = w_out.shape[0] // (2 * HEAD_DIM)
    lam_init = 0.8 - 0.6 * math.exp(-0.3 * layer)
    ex = lambda a, c: jnp.exp(jnp.sum(a * c))
    lam = ex(lam_q1, lam_k1) - ex(lam_q2, lam_k2) + lam_init
    slopes = 2.0 ** (-8.0 * jnp.arange(1, n_heads + 1, dtype=F32) / n_heads)
    par = jnp.concatenate([lam[None], slopes]).astype(F32)
    gq = g_q.reshape(1, HEAD_DIM)
    gs = g_subln.reshape(1, 2 * HEAD_DIM)

    proj = _mm(h, w_in, out_dtype=F32, tm=1024, tn=512, out_head_major=True, name="diff_in")
    k_hm = _rms_glue(proj[2 * n_heads:2 * n_heads + 2], g_k)
    if pools is None:
        o_hm = _diff_prompt_attn(proj, k_hm, par, gq, gs, batch=batch, seq=t, n_heads=n_heads,
                                 out_scale=1.0 - lam_init)
    else:
        page = pools[0].shape[1]
        pk, pv = (p.reshape(p.shape[0], 2 * page, HEAD_DIM) for p in pools)
        slope_rows = jnp.repeat(slopes, t)[:, None]
        o_hm = _diff_sample_attn(proj, k_hm, pk, pv, page_table, par, slope_rows, gq, gs,
                                 batch=batch, t_dec=t, n_heads=n_heads, out_scale=1.0 - lam_init)
    o = _mm(o_hm, w_out, out_dtype=F32, tm=1024, tn=512, a_head_major=True, name="diff_out")
    k = _to_token_major(k_hm, batch, t).reshape(batch, t, 1, 2 * HEAD_DIM)
    v = _to_token_major(proj[2 * n_heads + 2:], batch, t).reshape(batch, t, 1, 2 * HEAD_DIM)
    return o, k, v


def kernel(x_prompt, x_sample, c_prompt, c_sample, cache_l0_k, cache_l0_v, cache_l1_ckv, cache_l1_kpe, cache_l2_k, cache_l2_v, cache_l3_k, cache_l3_v, page_table, w_ada, b_ada, ada_table, g_norm_mix, g_norm_ffn, w_in_sb, w_out_sb, w_in_mla, g_q_a, g_kv_a, g_k_pe, w_q_b, g_q_head, w_kv_b, w_out_mla, w_in_diff, g_q_diff, g_k_diff, lam_q1, lam_k1, lam_q2, lam_k2, g_subln, w_out_diff, w_gate_up, w_down):
    depth, n_chunks, d = ada_table.shape
    pools = ((cache_l0_k, cache_l0_v), (cache_l1_ckv, cache_l1_kpe),
             (cache_l2_k, cache_l2_v), (cache_l3_k, cache_l3_v))
    bf = lambda a: a.astype(BF16)

    n_p = c_prompt.shape[0]
    c_all = jnp.concatenate([c_prompt, c_sample], axis=0)
    silu_c = c_all * (1.0 / (1.0 + jnp.exp(-c_all)))
    mod_all = _mm(silu_c, bf(w_ada), out_dtype=F32, tm=c_all.shape[0], tn=512, name="ada") + b_ada
    mod_all = mod_all.reshape(c_all.shape[0], n_chunks, d)

    streams = [
        dict(x=x_prompt, mod=mod_all[:n_p], pools=None, bb=1, tt=256),
        dict(x=x_sample, mod=mod_all[n_p:], pools=pools, bb=32, tt=x_sample.shape[1]),
    ]

    q_rank, kv_rank, rope = g_q_a.shape[1], g_kv_a.shape[1], g_k_pe.shape[1]
    nope = g_q_head.shape[1] - rope

    new_rows = [[], []]
    pending = [None, None]
    for l in range(depth):
        kind, j = l % 3, l // 3
        w_gu, w_dn = bf(w_gate_up[l]), bf(w_down[l])
        if kind == 0:
            mix_w = (bf(w_in_sb[j]), bf(w_out_sb[j]))
        elif kind == 1:
            wkv = w_kv_b[j]
            mix_w = (bf(w_in_mla[j][:, :q_rank + kv_rank]), bf(w_in_mla[j][:, q_rank + kv_rank:]),
                     g_q_a[j], g_kv_a[j], g_k_pe[j], bf(w_q_b[j]), g_q_head[j],
                     bf(jnp.transpose(wkv[..., :nope], (1, 2, 0))),
                     bf(jnp.transpose(wkv[..., nope:], (1, 0, 2))),
                     bf(wkv.reshape(kv_rank, -1)),
                     bf(w_out_mla[j]))
        else:
            mix_w = (bf(w_in_diff[j]), g_q_diff[j], g_k_diff[j], lam_q1[j], lam_k1[j], lam_q2[j], lam_k2[j],
                     g_subln[j], bf(w_out_diff[j]))
        for si, st in enumerate(streams):
            x = st["x"]
            batch, t, _ = x.shape
            mod = st["mod"] + ada_table[l]
            chunk = lambda i: mod[:, i:i + 1, :]
            o_prev, gt_prev = pending[si] if pending[si] is not None else (None, None)
            x, h = _norm_stage(x, o_prev, gt_prev, g_norm_mix[l].reshape(1, 1, d), chunk(1), chunk(0),
                               bb=st["bb"], tt=st["tt"])
            tok_pools = None if st["pools"] is None else st["pools"][l]
            pt = None if st["pools"] is None else page_table
            if kind == 0:
                o, st_a, st_b = _sb_mixer(h, (batch, t), mix_w[0], mix_w[1], tok_pools, pt)
            elif kind == 1:
                o, st_a, st_b = _mla_mixer(h, (batch, t), mix_w, tok_pools, pt)
            else:
                o, st_a, st_b = _diff_mixer(h, (batch, t), l, mix_w, tok_pools, pt)
            new_rows[si] += [st_a, st_b]
            x, h = _norm_stage(x, o.reshape(batch, t, d), chunk(2), g_norm_ffn[l].reshape(1, 1, d),
                               chunk(4), chunk(3), bb=st["bb"], tt=st["tt"])
            o = _ffn(h, w_gu, w_dn, tm=512, tf=256)
            pending[si] = (o.reshape(batch, t, d), chunk(5))
            st["x"] = x

    outs = []
    for si, st in enumerate(streams):
        o, gt = pending[si]
        y, _ = _norm_stage(st["x"], o, gt, None, None, None, bb=st["bb"], tt=st["tt"])
        outs.append(y)
    return (outs[0], outs[1], *new_rows[0], *new_rows[1])
```
